```python
import math
import jax
import jax.numpy as jnp
from jax import lax
import numpy as np

D_MODEL = 4096
BATCH = 4
SEQ = 2048
DEPTH = 4
DEC_BATCH = 8
DEC_SEQ = 4
PAST_LEN = 8192
PAGE_SIZE = 128

D_GROUP = D_MODEL // 4
SSD_HEADDIM = 64
SSD_HEADS = D_GROUP // SSD_HEADDIM
SSD_STATE = 128
SSD_GROUPS = 2
SSD_CONV = 4
SSD_CHUNK = 128
SSD_CONV_DIM = D_GROUP + 2 * SSD_GROUPS * SSD_STATE
POOL_WINDOWS = (2, 4, 8, 16)
POOL_GROUP = D_GROUP // len(POOL_WINDOWS)
POOL_BUF = max(POOL_WINDOWS) - 1
S5_CH = 16
S5_GROUPS = D_GROUP // S5_CH
S5_STATE = 64
ATTN_HEADDIM = 128
ATTN_HEADS = D_GROUP // ATTN_HEADDIM
MOBA_BLOCK = 256
MOBA_TOPK = 3
MOBA_QCHUNK = 16
D_FF = 11008
N_EXPERTS = 8
TOP_K = 2
D_FF_EXPERT = D_FF // 2
N_DENSE = (DEPTH + 1) // 2
N_MOE = DEPTH // 2
EPS = 1e-6
OFF_Z = 0
OFF_XBC = OFF_Z + D_GROUP
OFF_DT = OFF_XBC + SSD_CONV_DIM
OFF_POOL = OFF_DT + SSD_HEADS
OFF_S5 = OFF_POOL + D_GROUP
OFF_Q = OFF_S5 + D_GROUP
OFF_K = OFF_Q + D_GROUP
OFF_V = OFF_K + D_GROUP
D_IN = OFF_V + D_GROUP

kernel_name = 'hymba_ssd_pool_s5_moba_decoder_step'


def rmsnorm(x, g):
    xf = x.astype(jnp.float32)
    xf = xf * lax.rsqrt(jnp.mean(xf * xf, axis=-1, keepdims=True) + EPS)
    return (xf * g.astype(jnp.float32)).astype(x.dtype)


def alibi_slopes(n):
    return 2.0 ** (-8.0 * jnp.arange(1, n + 1, dtype=jnp.float32) / n)


def causal_depthwise_conv(ext, w, bias, T):
    out = bias
    for k in range(w.shape[0]):
        out = out + ext[:, k:k + T] * w[k]
    return out


def ssd_chunked_scan(x, dt, a, bm, cm, h0, chunk):
    b, T, H, P = x.shape
    nc = T // chunk
    rep = H // SSD_GROUPS
    f32 = jnp.float32
    bh = jnp.repeat(bm, rep, axis=2).astype(f32)
    ch = jnp.repeat(cm, rep, axis=2).astype(f32)
    da = dt * a
    resh = lambda t: t.reshape((b, nc, chunk) + t.shape[2:])
    xc, dtc, ac, bc, cc = resh(x.astype(f32)), resh(dt), resh(da), resh(bh), resh(ch)
    acum = jnp.cumsum(ac, axis=2)
    diff = acum[:, :, :, None, :] - acum[:, :, None, :, :]
    causal = jnp.tril(jnp.ones((chunk, chunk), dtype=bool))
    decay = jnp.exp(jnp.where(causal[None, None, :, :, None], diff, -jnp.inf))
    scores = jnp.einsum('bcihn,bcjhn->bcijh', cc, bc)
    y_diag = jnp.einsum('bcijh,bcjhp->bcihp', scores * decay * dtc[:, :, None], xc)
    decay_end = jnp.exp(acum[:, :, -1:, :] - acum)
    states = jnp.einsum('bcjhn,bcjh,bcjhp->bchpn', bc, decay_end * dtc, xc)
    chunk_decay = jnp.exp(acum[:, :, -1, :])

    def step(h, inp):
        dec, st = inp
        return h * dec[:, :, None, None] + st, h

    h_last, h_start = lax.scan(step, h0.astype(f32),
                               (jnp.moveaxis(chunk_decay, 1, 0), jnp.moveaxis(states, 1, 0)))
    h_start = jnp.moveaxis(h_start, 0, 1)
    y_off = jnp.einsum('bcihn,bchpn->bcihp', cc, h_start) * jnp.exp(acum)[..., None]
    y = (y_diag + y_off).reshape(b, T, H, P)
    return y.astype(x.dtype), h_last.astype(h0.dtype)


def ssd_mixer(z, xbc, dt_raw, conv_buf, h0, conv_w, conv_b, dt_bias, a_log, d_skip, norm_g):
    b, T, _ = xbc.shape
    ext = jnp.concatenate([conv_buf.astype(xbc.dtype), xbc], axis=1)
    new_buf = ext[:, T:]
    xbc_c = jax.nn.silu(causal_depthwise_conv(ext, conv_w, conv_b, T))
    gn = SSD_GROUPS * SSD_STATE
    xs = xbc_c[..., :D_GROUP].reshape(b, T, SSD_HEADS, SSD_HEADDIM)
    bm = xbc_c[..., D_GROUP:D_GROUP + gn].reshape(b, T, SSD_GROUPS, SSD_STATE)
    cm = xbc_c[..., D_GROUP + gn:].reshape(b, T, SSD_GROUPS, SSD_STATE)
    dt = jax.nn.softplus((dt_raw + dt_bias).astype(jnp.float32))
    a = -jnp.exp(a_log.astype(jnp.float32))
    chunk = SSD_CHUNK if T % SSD_CHUNK == 0 else T
    y, h_last = ssd_chunked_scan(xs, dt, a, bm, cm, h0, chunk)
    y = (y + xs * d_skip[:, None]).reshape(b, T, D_GROUP)
    y = rmsnorm(y * jax.nn.silu(z), norm_g)
    return y, new_buf, h_last


def pool_mixer(u, buf, pos0, pool_w, pool_scale):
    b, T, C = u.shape
    ext = jnp.concatenate([buf.astype(u.dtype), u], axis=1)
    new_buf = ext[:, T:]
    cs = jnp.cumsum(ext.astype(jnp.float32), axis=1)
    cs = jnp.concatenate([jnp.zeros_like(cs[:, :1]), cs], axis=1)
    pos = pos0 + jnp.arange(T, dtype=jnp.int32)
    outs = []
    for g, w in enumerate(POOL_WINDOWS):
        sl = slice(g * POOL_GROUP, (g + 1) * POOL_GROUP)
        end = cs[:, POOL_BUF + 1:POOL_BUF + 1 + T, sl]
        start = cs[:, POOL_BUF + 1 - w:POOL_BUF + 1 - w + T, sl]
        cnt = jnp.minimum(w, pos + 1).astype(jnp.float32)[None, :, None]
        d = ((end - start) / cnt - u[..., sl].astype(jnp.float32)).astype(u.dtype)
        outs.append(d @ pool_w[g])
    return jnp.concatenate(outs, axis=-1) * pool_scale, new_buf


def s5_mixer(u, h0_re, h0_im, a_re, a_im, log_dt, b_re, b_im, c_re, c_im, d_skip, glu_w, glu_b):
    b, T, C = u.shape
    f32 = jnp.float32
    ug = u.reshape(b, T, S5_GROUPS, S5_CH).astype(f32)
    dt = jnp.exp(log_dt.astype(f32))[:, None]
    ar, ai = a_re.astype(f32), a_im.astype(f32)
    mag = jnp.exp(dt * ar)
    abar_re, abar_im = mag * jnp.cos(dt * ai), mag * jnp.sin(dt * ai)
    den = ar * ar + ai * ai
    nr, ni = abar_re - 1.0, abar_im
    f_re, f_im = (nr * ar + ni * ai) / den, (ni * ar - nr * ai) / den
    bb_re = f_re[..., None] * b_re - f_im[..., None] * b_im
    bb_im = f_re[..., None] * b_im + f_im[..., None] * b_re
    bu_re = jnp.einsum('gnc,btgc->btgn', bb_re, ug)
    bu_im = jnp.einsum('gnc,btgc->btgn', bb_im, ug)
    a_full_re = jnp.broadcast_to(abar_re, bu_re.shape)
    a_full_im = jnp.broadcast_to(abar_im, bu_im.shape)

    def combine(e1, e2):
        a1r, a1i, b1r, b1i = e1
        a2r, a2i, b2r, b2i = e2
        return (a2r * a1r - a2i * a1i, a2r * a1i + a2i * a1r,
                a2r * b1r - a2i * b1i + b2r, a2r * b1i + a2i * b1r + b2i)

    _, _, hr, hi = lax.associative_scan(combine, (a_full_re, a_full_im, bu_re, bu_im), axis=1)
    k = jnp.arange(1, T + 1, dtype=f32)[:, None, None]
    pm = jnp.exp(k * dt * ar)
    pw_re, pw_im = pm * jnp.cos(k * dt * ai), pm * jnp.sin(k * dt * ai)
    h0r, h0i = h0_re.astype(f32)[:, None], h0_im.astype(f32)[:, None]
    hr = hr + pw_re * h0r - pw_im * h0i
    hi = hi + pw_re * h0i + pw_im * h0r
    y = jnp.einsum('gcn,btgn->btgc', c_re, hr) - jnp.einsum('gcn,btgn->btgc', c_im, hi)
    y = y.reshape(b, T, C) + d_skip * u.astype(f32)
    y = jax.nn.gelu(y)
    y = y * jax.nn.sigmoid(y @ glu_w.astype(f32) + glu_b)
    return y.astype(u.dtype), hr[:, -1].astype(h0_re.dtype), hi[:, -1].astype(h0_im.dtype)


def moba_attend(q, k_all, v_all, q_pos):
    b, Tq, H, Dh = q.shape
    L = k_all.shape[1]
    nb = -(-L // MOBA_BLOCK)
    pad = nb * MOBA_BLOCK - L
    padw = ((0, 0), (0, pad), (0, 0), (0, 0))
    kb = jnp.pad(k_all, padw).reshape(b, nb, MOBA_BLOCK, H, Dh).transpose(0, 3, 1, 2, 4)
    vb = jnp.pad(v_all, padw).reshape(b, nb, MOBA_BLOCK, H, Dh).transpose(0, 3, 1, 2, 4)
    kmean = jnp.mean(kb.astype(jnp.float32), axis=3)
    slopes = alibi_slopes(H)
    scale = 1.0 / math.sqrt(Dh)
    bi = jnp.arange(b)[:, None, None, None]
    hi = jnp.arange(H)[None, :, None, None]
    blk_ids = jnp.arange(nb, dtype=jnp.int32)
    offs = jnp.arange(MOBA_BLOCK, dtype=jnp.int32)

    def attend_chunk(args):
        qc, pc = args
        n_q = pc.shape[0]
        own = pc // MOBA_BLOCK
        gate = jnp.einsum('bhqd,bhnd->bhqn', qc.astype(jnp.float32), kmean)
        gate = jnp.where(blk_ids[None, :] < own[:, None], gate, -jnp.inf)
        if nb < MOBA_TOPK:
            gate = jnp.pad(gate, ((0, 0), (0, 0), (0, 0), (0, MOBA_TOPK - nb)), constant_values=-jnp.inf)
        g_val, g_idx = lax.top_k(gate, MOBA_TOPK)
        sel = jnp.concatenate([jnp.minimum(g_idx, nb - 1),
                               jnp.broadcast_to(own[None, None, :, None], (b, H, n_q, 1))], axis=-1)
        sel_ok = jnp.concatenate([jnp.isfinite(g_val), jnp.ones((b, H, n_q, 1), dtype=bool)], axis=-1)
        kg = kb[bi, hi, sel]
        vg = vb[bi, hi, sel]
        s = jnp.einsum('bhqd,bhqsnd->bhqsn', qc, kg).astype(jnp.float32) * scale
        kpos = sel[..., None] * MOBA_BLOCK + offs
        qpos = pc[None, None, :, None, None]
        s = s - slopes[None, :, None, None, None] * (qpos - kpos).astype(jnp.float32)
        s = jnp.where(sel_ok[..., None] & (kpos <= qpos), s, -jnp.inf)
        p = jax.nn.softmax(s.reshape(b, H, n_q, -1), axis=-1).reshape(s.shape)
        return jnp.einsum('bhqsn,bhqsnd->bhqd', p.astype(vg.dtype), vg)

    qlen = MOBA_QCHUNK if Tq % MOBA_QCHUNK == 0 else Tq
    nq = Tq // qlen
    qs = jnp.moveaxis(q.transpose(0, 2, 1, 3).reshape(b, H, nq, qlen, Dh), 2, 0)
    ps = q_pos.reshape(nq, qlen)
    o = lax.map(attend_chunk, (qs, ps))
    return jnp.moveaxis(o, 0, 2).reshape(b, H, Tq, Dh).transpose(0, 2, 1, 3)


def moba_mixer(q, k, v, k_past, v_past, pos0, q_g, k_g):
    b, T, _ = q.shape
    shp = (b, T, ATTN_HEADS, ATTN_HEADDIM)
    q = rmsnorm(q.reshape(shp), q_g)
    k = rmsnorm(k.reshape(shp), k_g)
    v = v.reshape(shp)
    if k_past is None:
        k_all, v_all = k, v
    else:
        k_all = jnp.concatenate([k_past.astype(k.dtype), k], axis=1)
        v_all = jnp.concatenate([v_past.astype(v.dtype), v], axis=1)
    o = moba_attend(q, k_all, v_all, pos0 + jnp.arange(T, dtype=jnp.int32))
    return o.reshape(b, T, D_GROUP), k, v


def swiglu(h, w1, w3, w2):
    return (jax.nn.silu(h @ w1) * (h @ w3)) @ w2


def moe_ffn(h, router_w, router_b, w1, w3, w2):
    logits = h.astype(jnp.float32) @ router_w.astype(jnp.float32) + router_b.astype(jnp.float32)
    top_v, top_i = lax.top_k(logits, TOP_K)
    gates = jax.nn.softmax(top_v, axis=-1)
    combine = jnp.sum(jax.nn.one_hot(top_i, N_EXPERTS, dtype=jnp.float32) * gates[..., None], axis=-2)
    out = jnp.zeros_like(h)
    for e in range(N_EXPERTS):
        out = out + combine[..., e:e + 1].astype(h.dtype) * swiglu(h, w1[e], w3[e], w2[e])
    return out


def run_trunk(x, c, pos0, p, cache, page_table):
    b, T, _ = x.shape
    new = [[] for _ in range(7)]
    for l in range(DEPTH):
        if cache is None:
            k_past = v_past = None
            ssd_h0 = jnp.zeros((b, SSD_HEADS, SSD_HEADDIM, SSD_STATE), x.dtype)
            conv_buf = jnp.zeros((b, SSD_CONV - 1, SSD_CONV_DIM), x.dtype)
            pool_buf = jnp.zeros((b, POOL_BUF, D_GROUP), x.dtype)
            s5_h0_re = jnp.zeros((b, S5_GROUPS, S5_STATE), x.dtype)
            s5_h0_im = jnp.zeros((b, S5_GROUPS, S5_STATE), x.dtype)
        else:
            cache_k, cache_v, st_ssd, st_conv, st_pool, st_re, st_im = cache
            k_past = cache_k[l][page_table].reshape(b, -1, ATTN_HEADS, ATTN_HEADDIM)
            v_past = cache_v[l][page_table].reshape(b, -1, ATTN_HEADS, ATTN_HEADDIM)
            ssd_h0, conv_buf, pool_buf = st_ssd[l], st_conv[l], st_pool[l]
            s5_h0_re, s5_h0_im = st_re[l], st_im[l]
        mod = jax.nn.silu(c) @ p['ada_w'][l] + p['ada_b'][l]
        sh1, sc1, g1, sh2, sc2, g2 = jnp.split(mod[:, None, :], 6, axis=-1)
        h = rmsnorm(x, p['norm1_g'][l]) * (1.0 + sc1) + sh1
        proj = h @ p['w_in'][l]
        y_ssd, conv_new, ssd_new = ssd_mixer(
            proj[..., OFF_Z:OFF_XBC], proj[..., OFF_XBC:OFF_DT], proj[..., OFF_DT:OFF_POOL],
            conv_buf, ssd_h0, p['ssd_conv_w'][l], p['ssd_conv_b'][l], p['ssd_dt_bias'][l],
            p['ssd_a_log'][l], p['ssd_d'][l], p['ssd_norm_g'][l])
        y_pool, pool_new = pool_mixer(proj[..., OFF_POOL:OFF_S5], pool_buf, pos0,
                                      p['pool_w'][l], p['pool_scale'][l])
        y_s5, re_new, im_new = s5_mixer(
            proj[..., OFF_S5:OFF_Q], s5_h0_re, s5_h0_im, p['s5_a_re'][l], p['s5_a_im'][l],
            p['s5_log_dt'][l], p['s5_b_re'][l], p['s5_b_im'][l], p['s5_c_re'][l], p['s5_c_im'][l],
            p['s5_d'][l], p['s5_glu_w'][l], p['s5_glu_b'][l])
        y_att, k_new, v_new = moba_mixer(
            proj[..., OFF_Q:OFF_K], proj[..., OFF_K:OFF_V], proj[..., OFF_V:D_IN],
            k_past, v_past, pos0, p['attn_q_g'][l], p['attn_k_g'][l])
        mixed = jnp.concatenate([y_ssd,
                                 rmsnorm(y_pool, p['pool_norm_g'][l]),
                                 rmsnorm(y_s5, p['s5_norm_g'][l]),
                                 rmsnorm(y_att, p['attn_norm_g'][l])], axis=-1) @ p['w_out'][l]
        x = x + g1 * mixed
        h2 = rmsnorm(x, p['norm2_g'][l]) * (1.0 + sc2) + sh2
        if l % 2 == 0:
            f = swiglu(h2, p['ffn_w1'][l // 2], p['ffn_w3'][l // 2], p['ffn_w2'][l // 2])
        else:
            f = moe_ffn(h2, p['moe_router_w'][l // 2], p['moe_router_b'][l // 2],
                        p['moe_w1'][l // 2], p['moe_w3'][l // 2], p['moe_w2'][l // 2])
        x = x + g2 * f
        for lst, val in zip(new, (k_new, v_new, ssd_new, conv_new, pool_new, re_new, im_new)):
            lst.append(val)
    return x, [jnp.stack(v, axis=0) for v in new]


def setup_inputs(seed: int = 0) -> dict:
    key = jax.random.key(seed)
    ks = iter(jax.random.split(key, 96))
    f32 = jnp.float32

    def nrm(shape, s):
        return jax.random.normal(next(ks), shape, f32) * s

    def gain(shape):
        return 1.0 + nrm(shape, 0.1)

    n_pages = PAST_LEN // PAGE_SIZE
    n_pool = (5 * DEC_BATCH * n_pages) // 4
    page_table = jax.random.permutation(next(ks), n_pool)[:DEC_BATCH * n_pages]
    page_table = page_table.reshape(DEC_BATCH, n_pages).astype(jnp.int32)
    log_lo, log_hi = math.log(1e-3), math.log(1e-1)
    ssd_dt = jnp.exp(jax.random.uniform(next(ks), (DEPTH, SSD_HEADS), f32, log_lo, log_hi))
    a_im0 = jnp.pi * jnp.arange(S5_STATE, dtype=f32)
    return {
        'x_prompt': nrm((BATCH, SEQ, D_MODEL), 1.0),
        'x_sample': nrm((DEC_BATCH, DEC_SEQ, D_MODEL), 1.0),
        'cache_k': nrm((DEPTH, n_pool, PAGE_SIZE, ATTN_HEADS, ATTN_HEADDIM), 1.0),
        'cache_v': nrm((DEPTH, n_pool, PAGE_SIZE, ATTN_HEADS, ATTN_HEADDIM), 1.0),
        'state_ssd': nrm((DEPTH, DEC_BATCH, SSD_HEADS, SSD_HEADDIM, SSD_STATE), 0.1),
        'state_ssd_conv': nrm((DEPTH, DEC_BATCH, SSD_CONV - 1, SSD_CONV_DIM), 1.0),
        'state_pool': nrm((DEPTH, DEC_BATCH, POOL_BUF, D_GROUP), 1.0),
        'state_s5_re': nrm((DEPTH, DEC_BATCH, S5_GROUPS, S5_STATE), 0.1),
        'state_s5_im': nrm((DEPTH, DEC_BATCH, S5_GROUPS, S5_STATE), 0.1),
        'page_table': page_table,
        'c_prompt': nrm((BATCH, D_MODEL), 1.0),
        'c_sample': nrm((DEC_BATCH, D_MODEL), 1.0),
        'ada_w': nrm((DEPTH, D_MODEL, 6 * D_MODEL), 0.3 * D_MODEL ** -0.5),
        'ada_b': nrm((DEPTH, 6 * D_MODEL), 0.01),
        'norm1_g': gain((DEPTH, D_MODEL)),
        'norm2_g': gain((DEPTH, D_MODEL)),
        'w_in': nrm((DEPTH, D_MODEL, D_IN), D_MODEL ** -0.5),
        'w_out': nrm((DEPTH, D_MODEL, D_MODEL), D_MODEL ** -0.5),
        'ssd_conv_w': nrm((DEPTH, SSD_CONV, SSD_CONV_DIM), SSD_CONV ** -0.5),
        'ssd_conv_b': nrm((DEPTH, SSD_CONV_DIM), 0.01),
        'ssd_dt_bias': ssd_dt + jnp.log(-jnp.expm1(-ssd_dt)),
        'ssd_a_log': jnp.log(jax.random.uniform(next(ks), (DEPTH, SSD_HEADS), f32, 1.0, 16.0)),
        'ssd_d': gain((DEPTH, SSD_HEADS)),
        'ssd_norm_g': gain((DEPTH, D_GROUP)),
        'pool_w': nrm((DEPTH, len(POOL_WINDOWS), POOL_GROUP, POOL_GROUP), POOL_GROUP ** -0.5),
        'pool_scale': gain((DEPTH, D_GROUP)),
        'pool_norm_g': gain((DEPTH, D_GROUP)),
        's5_a_re': -0.5 + nrm((DEPTH, S5_GROUPS, S5_STATE), 0.01),
        's5_a_im': a_im0 + nrm((DEPTH, S5_GROUPS, S5_STATE), 0.01),
        's5_log_dt': jax.random.uniform(next(ks), (DEPTH, S5_GROUPS), f32, log_lo, log_hi),
        's5_b_re': nrm((DEPTH, S5_GROUPS, S5_STATE, S5_CH), (2 * S5_CH) ** -0.5),
        's5_b_im': nrm((DEPTH, S5_GROUPS, S5_STATE, S5_CH), (2 * S5_CH) ** -0.5),
        's5_c_re': nrm((DEPTH, S5_GROUPS, S5_CH, S5_STATE), (2 * S5_STATE) ** -0.5),
        's5_c_im': nrm((DEPTH, S5_GROUPS, S5_CH, S5_STATE), (2 * S5_STATE) ** -0.5),
        's5_d': gain((DEPTH, D_GROUP)),
        's5_glu_w': nrm((DEPTH, D_GROUP, D_GROUP), D_GROUP ** -0.5),
        's5_glu_b': nrm((DEPTH, D_GROUP), 0.01),
        's5_norm_g': gain((DEPTH, D_GROUP)),
        'attn_q_g': gain((DEPTH, ATTN_HEADDIM)),
        'attn_k_g': gain((DEPTH, ATTN_HEADDIM)),
        'attn_norm_g': gain((DEPTH, D_GROUP)),
        'ffn_w1': nrm((N_DENSE, D_MODEL, D_FF), D_MODEL ** -0.5),
        'ffn_w3': nrm((N_DENSE, D_MODEL, D_FF), D_MODEL ** -0.5),
        'ffn_w2': nrm((N_DENSE, D_FF, D_MODEL), D_FF ** -0.5),
        'moe_router_w': nrm((N_MOE, D_MODEL, N_EXPERTS), D_MODEL ** -0.5),
        'moe_router_b': nrm((N_MOE, N_EXPERTS), 0.01),
        'moe_w1': nrm((N_MOE, N_EXPERTS, D_MODEL, D_FF_EXPERT), D_MODEL ** -0.5),
        'moe_w3': nrm((N_MOE, N_EXPERTS, D_MODEL, D_FF_EXPERT), D_MODEL ** -0.5),
        'moe_w2': nrm((N_MOE, N_EXPERTS, D_FF_EXPERT, D_MODEL), D_FF_EXPERT ** -0.5),
    }


def reference(x_prompt, x_sample, cache_k, cache_v, state_ssd, state_ssd_conv, state_pool,
              state_s5_re, state_s5_im, page_table, c_prompt, c_sample,
              ada_w, ada_b, norm1_g, norm2_g, w_in, w_out,
              ssd_conv_w, ssd_conv_b, ssd_dt_bias, ssd_a_log, ssd_d, ssd_norm_g,
              pool_w, pool_scale, pool_norm_g,
              s5_a_re, s5_a_im, s5_log_dt, s5_b_re, s5_b_im, s5_c_re, s5_c_im,
              s5_d, s5_glu_w, s5_glu_b, s5_norm_g,
              attn_q_g, attn_k_g, attn_norm_g,
              ffn_w1, ffn_w3, ffn_w2,
              moe_router_w, moe_router_b, moe_w1, moe_w3, moe_w2):
    p = dict(ada_w=ada_w, ada_b=ada_b, norm1_g=norm1_g, norm2_g=norm2_g, w_in=w_in, w_out=w_out,
             ssd_conv_w=ssd_conv_w, ssd_conv_b=ssd_conv_b, ssd_dt_bias=ssd_dt_bias,
             ssd_a_log=ssd_a_log, ssd_d=ssd_d, ssd_norm_g=ssd_norm_g,
             pool_w=pool_w, pool_scale=pool_scale, pool_norm_g=pool_norm_g,
             s5_a_re=s5_a_re, s5_a_im=s5_a_im, s5_log_dt=s5_log_dt, s5_b_re=s5_b_re, s5_b_im=s5_b_im,
             s5_c_re=s5_c_re, s5_c_im=s5_c_im, s5_d=s5_d, s5_glu_w=s5_glu_w, s5_glu_b=s5_glu_b,
             s5_norm_g=s5_norm_g, attn_q_g=attn_q_g, attn_k_g=attn_k_g, attn_norm_g=attn_norm_g,
             ffn_w1=ffn_w1, ffn_w3=ffn_w3, ffn_w2=ffn_w2,
             moe_router_w=moe_router_w, moe_router_b=moe_router_b,
             moe_w1=moe_w1, moe_w3=moe_w3, moe_w2=moe_w2)
    y_prompt, st_p = run_trunk(x_prompt, c_prompt, 0, p, None, None)
    cache = (cache_k, cache_v, state_ssd, state_ssd_conv, state_pool, state_s5_re, state_s5_im)
    y_sample, st_s = run_trunk(x_sample, c_sample, PAST_LEN, p, cache, page_table)
    k_p, v_p, ssd_p, conv_p, pool_p, re_p, im_p = st_p
    k_s, v_s, ssd_s, conv_s, pool_s, re_s, im_s = st_s
    return (y_prompt, y_sample, k_p, v_p, ssd_p, conv_p, pool_p, re_p, im_p,
            k_s, v_s, ssd_s, conv_s, pool_s, re_s, im_s)
```

```python
import functools
import math

import jax
import jax.numpy as jnp
from jax import lax
from jax.experimental import pallas as pl
from jax.experimental.pallas import tpu as pltpu

D_MODEL = 4096
BATCH = 4
SEQ = 2048
DEPTH = 4
DEC_BATCH = 8
DEC_SEQ = 4
PAST_LEN = 8192
PAGE_SIZE = 128
D_GROUP = D_MODEL // 4
SSD_HEADDIM = 64
SSD_HEADS = D_GROUP // SSD_HEADDIM
SSD_STATE = 128
SSD_GROUPS = 2
SSD_CONV = 4
SSD_CHUNK = 128
SSD_CONV_DIM = D_GROUP + 2 * SSD_GROUPS * SSD_STATE
POOL_WINDOWS = (2, 4, 8, 16)
POOL_GROUP = D_GROUP // len(POOL_WINDOWS)
POOL_BUF = max(POOL_WINDOWS) - 1
S5_CH = 16
S5_GROUPS = D_GROUP // S5_CH
S5_STATE = 64
ATTN_HEADDIM = 128
ATTN_HEADS = D_GROUP // ATTN_HEADDIM
MOBA_BLOCK = 256
MOBA_TOPK = 3
MOBA_QCHUNK = 16
D_FF = 11008
N_EXPERTS = 8
TOP_K = 2
D_FF_EXPERT = D_FF // 2
EPS = 1e-6
OFF_Z = 0
OFF_XBC = OFF_Z + D_GROUP
OFF_DT = OFF_XBC + SSD_CONV_DIM
OFF_POOL = OFF_DT + SSD_HEADS
OFF_S5 = OFF_POOL + D_GROUP
OFF_Q = OFF_S5 + D_GROUP
OFF_K = OFF_Q + D_GROUP
OFF_V = OFF_K + D_GROUP
D_IN = OFF_V + D_GROUP

F32 = jnp.float32
BF16 = jnp.bfloat16
LANES = 128
BF16_SUBLANES = 16
VMEM_LIMIT_BYTES = 56 * 1024 * 1024


def _gmm_kernel(*refs, n_w, has_res, has_gate, has_partial, has_rowscale, has_valid):
    it = iter(refs)
    te_ref, tv_ref, x_ref = next(it), next(it), next(it)
    w_refs = [next(it) for _ in range(n_w)]
    res_ref = next(it) if has_res else None
    gate_ref = next(it) if has_gate else None
    partial_ref = next(it) if has_partial else None
    rowscale_ref = next(it) if has_rowscale else None
    o_ref = next(it)
    wbf_refs = [next(it) for _ in range(n_w)]

    i = pl.program_id(1)
    prev = te_ref[jnp.maximum(i - 1, 0)]
    group_changed = jnp.logical_or(i == 0, te_ref[i] != prev)

    @pl.when(group_changed)
    def _():
        for w_ref, wbf_ref in zip(w_refs, wbf_refs):
            wbf_ref[...] = w_ref[0].astype(BF16)

    def compute():
        x = x_ref[...]
        acc = jnp.dot(x, wbf_refs[0][...], preferred_element_type=F32)
        if n_w == 2:
            up = jnp.dot(x, wbf_refs[1][...], preferred_element_type=F32)
            acc = (acc * (1.0 / (1.0 + jnp.exp(-acc)))) * up
        if has_partial:
            acc = partial_ref[...] + acc
        if has_rowscale:
            acc = rowscale_ref[...] * acc
        if has_gate:
            acc = gate_ref[0] * acc
        if has_res:
            acc = res_ref[...] + acc
        o_ref[...] = acc.astype(o_ref.dtype)

    if has_valid:
        pl.when(tv_ref[i] != 0)(compute)

        @pl.when(tv_ref[i] == 0)
        def _():
            o_ref[...] = jnp.zeros_like(o_ref)
    else:
        compute()


def _gmm(x, ws, tile_group, *, tm, tn, out_dtype, n_cols=None, x_col_block=0, k_dim=None,
         res=None, gate=None, rows_per_gate=None, partial=None, rowscale=None, tile_valid=None):
    p_rows = x.shape[0]
    k = ws[0].shape[1] if k_dim is None else k_dim
    n_cols = ws[0].shape[2] if n_cols is None else n_cols
    n_row_tiles = p_rows // tm
    assert n_row_tiles * tm == p_rows and tile_group.shape == (n_row_tiles,)
    has_valid = tile_valid is not None
    if tile_valid is None:
        tile_valid = jnp.ones((n_row_tiles,), jnp.int32)
    grid = (pl.cdiv(n_cols, tn), n_row_tiles)

    in_specs = [pl.BlockSpec((tm, k), lambda j, i, te, tv: (i, x_col_block))]
    args = [x]
    for w in ws:
        in_specs.append(pl.BlockSpec((1, k, tn), lambda j, i, te, tv: (te[i], 0, j)))
        args.append(w)
    if res is not None:
        in_specs.append(pl.BlockSpec((tm, tn), lambda j, i, te, tv: (i, j)))
        args.append(res)
    if gate is not None:
        rg = gate.shape[1]
        in_specs.append(pl.BlockSpec((1, rg, tn), lambda j, i, te, tv: ((i * tm) // rows_per_gate, 0, j)))
        args.append(gate)
    if partial is not None:
        in_specs.append(pl.BlockSpec((tm, tn), lambda j, i, te, tv: (i, j)))
        args.append(partial)
    if rowscale is not None:
        in_specs.append(pl.BlockSpec((tm, 1), lambda j, i, te, tv: (i, 0)))
        args.append(rowscale)

    kernel = functools.partial(
        _gmm_kernel, n_w=len(ws), has_res=res is not None, has_gate=gate is not None,
        has_partial=partial is not None, has_rowscale=rowscale is not None, has_valid=has_valid)
    return pl.pallas_call(
        kernel,
        grid_spec=pltpu.PrefetchScalarGridSpec(
            num_scalar_prefetch=2,
            grid=grid,
            in_specs=in_specs,
            out_specs=pl.BlockSpec((tm, tn), lambda j, i, te, tv: (i, j)),
            scratch_shapes=[pltpu.VMEM((k, tn), BF16) for _ in ws],
        ),
        out_shape=jax.ShapeDtypeStruct((p_rows, n_cols), out_dtype),
        compiler_params=pltpu.CompilerParams(
            dimension_semantics=("arbitrary", "arbitrary"),
            vmem_limit_bytes=VMEM_LIMIT_BYTES),
    )(tile_group, tile_valid, *args)


def _norm_mod_kernel(*refs, has_router):
    if has_router:
        x_ref, g_ref, sc_ref, sh_ref, rw_ref, h_ref, logit_ref = refs
    else:
        x_ref, g_ref, sc_ref, sh_ref, h_ref = refs
    xf = x_ref[...]
    ms = jnp.mean(xf * xf, axis=-1, keepdims=True)
    y = xf * lax.rsqrt(ms + EPS) * g_ref[...]
    h = y * (1.0 + sc_ref[0]) + sh_ref[0]
    h_ref[...] = h.astype(BF16)
    if has_router:
        logit_ref[...] = jnp.dot(h, rw_ref[...], preferred_element_type=F32,
                                 precision=lax.Precision.HIGHEST)


def _norm_mod(x, g, sc, sh, *, tr, rows_per_gate, router_w=None):
    r, d = x.shape
    rg = sc.shape[1]
    mod_spec = pl.BlockSpec((1, rg, d), lambda i: ((i * tr) // rows_per_gate, 0, 0))
    in_specs = [pl.BlockSpec((tr, d), lambda i: (i, 0)), pl.BlockSpec((1, d), lambda i: (0, 0)),
                mod_spec, mod_spec]
    args = [x, g.reshape(1, d), sc, sh]
    out_shape = [jax.ShapeDtypeStruct((r, d), BF16)]
    out_specs = [pl.BlockSpec((tr, d), lambda i: (i, 0))]
    if router_w is not None:
        rw = jnp.pad(router_w, ((0, 0), (0, LANES - router_w.shape[1])))
        in_specs.append(pl.BlockSpec((d, LANES), lambda i: (0, 0)))
        args.append(rw)
        out_shape.append(jax.ShapeDtypeStruct((r, LANES), F32))
        out_specs.append(pl.BlockSpec((tr, LANES), lambda i: (i, 0)))
    out = pl.pallas_call(
        functools.partial(_norm_mod_kernel, has_router=router_w is not None),
        grid=(r // tr,),
        in_specs=in_specs,
        out_specs=out_specs,
        out_shape=out_shape,
        compiler_params=pltpu.CompilerParams(
            dimension_semantics=("arbitrary",), vmem_limit_bytes=VMEM_LIMIT_BYTES),
    )(*args)
    return out if router_w is not None else out[0]


def _rmsnorm(x, g):
    xf = x.astype(F32)
    xf = xf * lax.rsqrt(jnp.mean(xf * xf, axis=-1, keepdims=True) + EPS)
    return (xf * g.astype(F32)).astype(x.dtype)


def _alibi_slopes(n):
    return 2.0 ** (-8.0 * jnp.arange(1, n + 1, dtype=F32) / n)


def _causal_depthwise_conv(ext, w, bias, t):
    out = bias
    for k in range(w.shape[0]):
        out = out + ext[:, k:k + t] * w[k]
    return out


def _ssd_chunked_scan(x, dt, a, bm, cm, h0, chunk):
    b, t, h, p = x.shape
    nc = t // chunk
    rep = h // SSD_GROUPS
    bh = jnp.repeat(bm, rep, axis=2).astype(F32)
    ch = jnp.repeat(cm, rep, axis=2).astype(F32)
    da = dt * a
    resh = lambda v: v.reshape((b, nc, chunk) + v.shape[2:])
    xc, dtc, ac, bc, cc = resh(x.astype(F32)), resh(dt), resh(da), resh(bh), resh(ch)
    acum = jnp.cumsum(ac, axis=2)
    diff = acum[:, :, :, None, :] - acum[:, :, None, :, :]
    causal = jnp.tril(jnp.ones((chunk, chunk), dtype=bool))
    decay = jnp.exp(jnp.where(causal[None, None, :, :, None], diff, -jnp.inf))
    scores = jnp.einsum('bcihn,bcjhn->bcijh', cc, bc)
    y_diag = jnp.einsum('bcijh,bcjhp->bcihp', scores * decay * dtc[:, :, None], xc)
    decay_end = jnp.exp(acum[:, :, -1:, :] - acum)
    states = jnp.einsum('bcjhn,bcjh,bcjhp->bchpn', bc, decay_end * dtc, xc)
    chunk_decay = jnp.exp(acum[:, :, -1, :])

    def step(hh, inp):
        dec, st = inp
        return hh * dec[:, :, None, None] + st, hh

    h_last, h_start = lax.scan(step, h0.astype(F32),
                               (jnp.moveaxis(chunk_decay, 1, 0), jnp.moveaxis(states, 1, 0)))
    h_start = jnp.moveaxis(h_start, 0, 1)
    y_off = jnp.einsum('bcihn,bchpn->bcihp', cc, h_start) * jnp.exp(acum)[..., None]
    y = (y_diag + y_off).reshape(b, t, h, p)
    return y.astype(x.dtype), h_last.astype(h0.dtype)


def _ssd_mixer(z, xbc, dt_raw, conv_buf, h0, conv_w, conv_b, dt_bias, a_log, d_skip, norm_g):
    b, t, _ = xbc.shape
    ext = jnp.concatenate([conv_buf.astype(xbc.dtype), xbc], axis=1)
    new_buf = ext[:, t:]
    xbc_c = jax.nn.silu(_causal_depthwise_conv(ext, conv_w, conv_b, t))
    gn = SSD_GROUPS * SSD_STATE
    xs = xbc_c[..., :D_GROUP].reshape(b, t, SSD_HEADS, SSD_HEADDIM)
    bm = xbc_c[..., D_GROUP:D_GROUP + gn].reshape(b, t, SSD_GROUPS, SSD_STATE)
    cm = xbc_c[..., D_GROUP + gn:].reshape(b, t, SSD_GROUPS, SSD_STATE)
    dt = jax.nn.softplus((dt_raw + dt_bias).astype(F32))
    a = -jnp.exp(a_log.astype(F32))
    chunk = SSD_CHUNK if t % SSD_CHUNK == 0 else t
    y, h_last = _ssd_chunked_scan(xs, dt, a, bm, cm, h0, chunk)
    y = (y + xs * d_skip[:, None]).reshape(b, t, D_GROUP)
    y = _rmsnorm(y * jax.nn.silu(z), norm_g)
    return y, new_buf, h_last


def _pool_mixer(u, buf, pos0, pool_w, pool_scale):
    b, t, c = u.shape
    ext = jnp.concatenate([buf.astype(u.dtype), u], axis=1)
    new_buf = ext[:, t:]
    cs = jnp.cumsum(ext.astype(F32), axis=1)
    cs = jnp.concatenate([jnp.zeros_like(cs[:, :1]), cs], axis=1)
    pos = pos0 + jnp.arange(t, dtype=jnp.int32)
    outs = []
    for g, w in enumerate(POOL_WINDOWS):
        sl = slice(g * POOL_GROUP, (g + 1) * POOL_GROUP)
        end = cs[:, POOL_BUF + 1:POOL_BUF + 1 + t, sl]
        start = cs[:, POOL_BUF + 1 - w:POOL_BUF + 1 - w + t, sl]
        cnt = jnp.minimum(w, pos + 1).astype(F32)[None, :, None]
        d = ((end - start) / cnt - u[..., sl].astype(F32)).astype(u.dtype)
        outs.append(d @ pool_w[g])
    return jnp.concatenate(outs, axis=-1) * pool_scale, new_buf


def _s5_mixer(u, h0_re, h0_im, a_re, a_im, log_dt, b_re, b_im, c_re, c_im, d_skip, glu_w, glu_b):
    b, t, c = u.shape
    ug = u.reshape(b, t, S5_GROUPS, S5_CH).astype(F32)
    dt = jnp.exp(log_dt.astype(F32))[:, None]
    ar, ai = a_re.astype(F32), a_im.astype(F32)
    mag = jnp.exp(dt * ar)
    abar_re, abar_im = mag * jnp.cos(dt * ai), mag * jnp.sin(dt * ai)
    den = ar * ar + ai * ai
    nr, ni = abar_re - 1.0, abar_im
    f_re, f_im = (nr * ar + ni * ai) / den, (ni * ar - nr * ai) / den
    bb_re = f_re[..., None] * b_re - f_im[..., None] * b_im
    bb_im = f_re[..., None] * b_im + f_im[..., None] * b_re
    bu_re = jnp.einsum('gnc,btgc->btgn', bb_re, ug)
    bu_im = jnp.einsum('gnc,btgc->btgn', bb_im, ug)
    a_full_re = jnp.broadcast_to(abar_re, bu_re.shape)
    a_full_im = jnp.broadcast_to(abar_im, bu_im.shape)

    def combine(e1, e2):
        a1r, a1i, b1r, b1i = e1
        a2r, a2i, b2r, b2i = e2
        return (a2r * a1r - a2i * a1i, a2r * a1i + a2i * a1r,
                a2r * b1r - a2i * b1i + b2r, a2r * b1i + a2i * b1r + b2i)

    _, _, hr, hi = lax.associative_scan(combine, (a_full_re, a_full_im, bu_re, bu_im), axis=1)
    k = jnp.arange(1, t + 1, dtype=F32)[:, None, None]
    pm = jnp.exp(k * dt * ar)
    pw_re, pw_im = pm * jnp.cos(k * dt * ai), pm * jnp.sin(k * dt * ai)
    h0r, h0i = h0_re.astype(F32)[:, None], h0_im.astype(F32)[:, None]
    hr = hr + pw_re * h0r - pw_im * h0i
    hi = hi + pw_re * h0i + pw_im * h0r
    y = jnp.einsum('gcn,btgn->btgc', c_re, hr) - jnp.einsum('gcn,btgn->btgc', c_im, hi)
    y = y.reshape(b, t, c) + d_skip * u.astype(F32)
    y = jax.nn.gelu(y)
    y = y * jax.nn.sigmoid(y @ glu_w.astype(F32) + glu_b)
    return y.astype(u.dtype), hr[:, -1].astype(h0_re.dtype), hi[:, -1].astype(h0_im.dtype)


def _moba_attend(q, k_all, v_all, q_pos):
    b, tq, h, dh = q.shape
    l = k_all.shape[1]
    nb = -(-l // MOBA_BLOCK)
    pad = nb * MOBA_BLOCK - l
    padw = ((0, 0), (0, pad), (0, 0), (0, 0))
    kb = jnp.pad(k_all, padw).reshape(b, nb, MOBA_BLOCK, h, dh).transpose(0, 3, 1, 2, 4)
    vb = jnp.pad(v_all, padw).reshape(b, nb, MOBA_BLOCK, h, dh).transpose(0, 3, 1, 2, 4)
    kmean = jnp.mean(kb.astype(F32), axis=3)
    slopes = _alibi_slopes(h)
    scale = 1.0 / math.sqrt(dh)
    bi = jnp.arange(b)[:, None, None, None]
    hi = jnp.arange(h)[None, :, None, None]
    blk_ids = jnp.arange(nb, dtype=jnp.int32)
    offs = jnp.arange(MOBA_BLOCK, dtype=jnp.int32)

    def attend_chunk(args):
        qc, pc = args
        n_q = pc.shape[0]
        own = pc // MOBA_BLOCK
        gate = jnp.einsum('bhqd,bhnd->bhqn', qc.astype(F32), kmean)
        gate = jnp.where(blk_ids[None, :] < own[:, None], gate, -jnp.inf)
        if nb < MOBA_TOPK:
            gate = jnp.pad(gate, ((0, 0), (0, 0), (0, 0), (0, MOBA_TOPK - nb)), constant_values=-jnp.inf)
        g_val, g_idx = lax.top_k(gate, MOBA_TOPK)
        sel = jnp.concatenate([jnp.minimum(g_idx, nb - 1),
                               jnp.broadcast_to(own[None, None, :, None], (b, h, n_q, 1))], axis=-1)
        sel_ok = jnp.concatenate([jnp.isfinite(g_val), jnp.ones((b, h, n_q, 1), dtype=bool)], axis=-1)
        kg = kb[bi, hi, sel]
        vg = vb[bi, hi, sel]
        s = jnp.einsum('bhqd,bhqsnd->bhqsn', qc, kg).astype(F32) * scale
        kpos = sel[..., None] * MOBA_BLOCK + offs
        qpos = pc[None, None, :, None, None]
        s = s - slopes[None, :, None, None, None] * (qpos - kpos).astype(F32)
        s = jnp.where(sel_ok[..., None] & (kpos <= qpos), s, -jnp.inf)
        p = jax.nn.softmax(s.reshape(b, h, n_q, -1), axis=-1).reshape(s.shape)
        return jnp.einsum('bhqsn,bhqsnd->bhqd', p.astype(vg.dtype), vg)

    qlen = MOBA_QCHUNK if tq % MOBA_QCHUNK == 0 else tq
    nq = tq // qlen
    qs = jnp.moveaxis(q.transpose(0, 2, 1, 3).reshape(b, h, nq, qlen, dh), 2, 0)
    ps = q_pos.reshape(nq, qlen)
    o = lax.map(attend_chunk, (qs, ps))
    return jnp.moveaxis(o, 0, 2).reshape(b, h, tq, dh).transpose(0, 2, 1, 3)


def _moba_mixer(q, k, v, k_past, v_past, pos0, q_g, k_g):
    b, t, _ = q.shape
    shp = (b, t, ATTN_HEADS, ATTN_HEADDIM)
    q = _rmsnorm(q.reshape(shp), q_g)
    k = _rmsnorm(k.reshape(shp), k_g)
    v = v.reshape(shp)
    if k_past is None:
        k_all, v_all = k, v
    else:
        k_all = jnp.concatenate([k_past.astype(k.dtype), k], axis=1)
        v_all = jnp.concatenate([v_past.astype(v.dtype), v], axis=1)
    o = _moba_attend(q, k_all, v_all, pos0 + jnp.arange(t, dtype=jnp.int32))
    return o.reshape(b, t, D_GROUP), k, v


def _dense_ffn(h2, x_res, g2, rows_per_gate, w1, w3, w2, l2, *, tm):
    rows = h2.shape[0]
    n_tiles = rows // tm
    grp = jnp.full((n_tiles,), l2, jnp.int32)
    act = _gmm(h2, [w1, w3], grp, tm=tm, tn=256, out_dtype=BF16)
    w2h = w2.reshape(w2.shape[0] * 2, D_FF_EXPERT, D_MODEL)
    part = _gmm(act, [w2h], grp * 2, tm=tm, tn=512, out_dtype=F32, x_col_block=0, k_dim=D_FF_EXPERT)
    return _gmm(act, [w2h], grp * 2 + 1, tm=tm, tn=512, out_dtype=F32, x_col_block=1, k_dim=D_FF_EXPERT,
                partial=part, gate=g2, rows_per_gate=rows_per_gate, res=x_res)


def _moe_ffn(h2, logits, x_res, g2_rows, router_b, w1, w3, w2, l2, *, tm):
    n_tok = h2.shape[0]
    lg = logits[:, :N_EXPERTS] + router_b.astype(F32)
    top_v, top_i = lax.top_k(lg, TOP_K)
    gates = jax.nn.softmax(top_v, axis=-1)

    n_pairs = n_tok * TOP_K
    n_tiles = -(-n_pairs // tm) + N_EXPERTS
    p_rows = n_tiles * tm
    e_flat = top_i.reshape(-1).astype(jnp.int32)
    onehot = (e_flat[:, None] == jnp.arange(N_EXPERTS, dtype=jnp.int32)[None, :]).astype(jnp.int32)
    rank = jnp.sum((jnp.cumsum(onehot, axis=0) - onehot) * onehot, axis=1)
    counts = jnp.sum(onehot, axis=0)
    tiles_e = (counts + tm - 1) // tm
    tile_end = jnp.cumsum(tiles_e)
    starts = (tile_end - tiles_e) * tm
    pos = starts[e_flat] + rank
    tok = jnp.arange(n_pairs, dtype=jnp.int32) // TOP_K
    row_token = jnp.zeros((p_rows,), jnp.int32).at[pos].set(tok)
    row_gate = jnp.zeros((p_rows,), F32).at[pos].set(gates.reshape(-1))
    tile_ids = jnp.arange(n_tiles, dtype=jnp.int32)
    total_tiles = tile_end[-1]
    tile_valid = (tile_ids < total_tiles).astype(jnp.int32)
    clipped = jnp.minimum(tile_ids, total_tiles - 1)
    tile_expert = jnp.sum((clipped[:, None] >= tile_end[None, :]).astype(jnp.int32), axis=1)
    tile_group = (l2 * N_EXPERTS + tile_expert).astype(jnp.int32)

    xs = jnp.take(h2, row_token, axis=0)
    n_moe = w1.shape[0]
    w1r = w1.reshape(n_moe * N_EXPERTS, D_MODEL, D_FF_EXPERT)
    w3r = w3.reshape(n_moe * N_EXPERTS, D_MODEL, D_FF_EXPERT)
    w2r = w2.reshape(n_moe * N_EXPERTS, D_FF_EXPERT, D_MODEL)
    act = _gmm(xs, [w1r, w3r], tile_group, tm=tm, tn=256, out_dtype=BF16, tile_valid=tile_valid)
    ys = _gmm(act, [w2r], tile_group, tm=tm, tn=512, out_dtype=F32,
              rowscale=row_gate.reshape(p_rows, 1), tile_valid=tile_valid)
    pos2 = pos.reshape(n_tok, TOP_K)
    f = jnp.take(ys, pos2[:, 0], axis=0) + jnp.take(ys, pos2[:, 1], axis=0)
    return x_res + g2_rows * f


def _run_trunk(x, mods, pos0, p, cache, page_table):
    b, t, _ = x.shape
    rows = b * t
    tm = 512 if rows % 512 == 0 else rows
    tr = 256 if rows % 256 == 0 else rows
    moe_tm = 512 if rows >= 4096 else BF16_SUBLANES
    n_tiles = rows // tm
    new = [[] for _ in range(7)]
    xr = x.reshape(rows, D_MODEL)
    for l in range(DEPTH):
        if cache is None:
            k_past = v_past = None
            ssd_h0 = jnp.zeros((b, SSD_HEADS, SSD_HEADDIM, SSD_STATE), x.dtype)
            conv_buf = jnp.zeros((b, SSD_CONV - 1, SSD_CONV_DIM), x.dtype)
            pool_buf = jnp.zeros((b, POOL_BUF, D_GROUP), x.dtype)
            s5_h0_re = jnp.zeros((b, S5_GROUPS, S5_STATE), x.dtype)
            s5_h0_im = jnp.zeros((b, S5_GROUPS, S5_STATE), x.dtype)
        else:
            cache_k, cache_v, st_ssd, st_conv, st_pool, st_re, st_im = cache
            k_past = cache_k[l][page_table].reshape(b, -1, ATTN_HEADS, ATTN_HEADDIM)
            v_past = cache_v[l][page_table].reshape(b, -1, ATTN_HEADS, ATTN_HEADDIM)
            ssd_h0, conv_buf, pool_buf = st_ssd[l], st_conv[l], st_pool[l]
            s5_h0_re, s5_h0_im = st_re[l], st_im[l]
        mod = mods[l]
        sh1, sc1, g1, sh2, sc2, g2 = [m.reshape(b, 1, D_MODEL) for m in jnp.split(mod, 6, axis=-1)]
        if t >= tm:
            gate_rows = t
            as_gate = lambda m: m
        else:
            gate_rows = rows
            as_gate = lambda m: jnp.broadcast_to(m, (b, t, D_MODEL)).reshape(1, rows, D_MODEL)
        grp = jnp.full((n_tiles,), l, jnp.int32)

        h = _norm_mod(xr, p['norm1_g'][l], as_gate(sc1), as_gate(sh1), tr=tr, rows_per_gate=gate_rows)
        proj_a = _gmm(h, [p['w_in']], grp, tm=tm, tn=512, out_dtype=F32, n_cols=OFF_DT)
        proj_b = _gmm(h, [p['w_in_b']], grp, tm=tm, tn=512, out_dtype=F32)
        proj_dt = _gmm(h, [p['w_in_dt']], grp, tm=tm, tn=LANES, out_dtype=F32)
        proj_a = proj_a.reshape(b, t, OFF_DT)
        proj_b = proj_b.reshape(b, t, D_IN - OFF_POOL)
        proj_dt = proj_dt.reshape(b, t, LANES)[..., :SSD_HEADS]
        seg = lambda i: proj_b[..., i * D_GROUP:(i + 1) * D_GROUP]

        y_ssd, conv_new, ssd_new = _ssd_mixer(
            proj_a[..., OFF_Z:OFF_XBC], proj_a[..., OFF_XBC:OFF_DT], proj_dt,
            conv_buf, ssd_h0, p['ssd_conv_w'][l], p['ssd_conv_b'][l], p['ssd_dt_bias'][l],
            p['ssd_a_log'][l], p['ssd_d'][l], p['ssd_norm_g'][l])
        y_pool, pool_new = _pool_mixer(seg(0), pool_buf, pos0, p['pool_w'][l], p['pool_scale'][l])
        y_s5, re_new, im_new = _s5_mixer(
            seg(1), s5_h0_re, s5_h0_im, p['s5_a_re'][l], p['s5_a_im'][l],
            p['s5_log_dt'][l], p['s5_b_re'][l], p['s5_b_im'][l], p['s5_c_re'][l], p['s5_c_im'][l],
            p['s5_d'][l], p['s5_glu_w'][l], p['s5_glu_b'][l])
        y_att, k_new, v_new = _moba_mixer(seg(2), seg(3), seg(4), k_past, v_past, pos0,
                                          p['attn_q_g'][l], p['attn_k_g'][l])
        mixed_in = jnp.concatenate([y_ssd,
                                    _rmsnorm(y_pool, p['pool_norm_g'][l]),
                                    _rmsnorm(y_s5, p['s5_norm_g'][l]),
                                    _rmsnorm(y_att, p['attn_norm_g'][l])], axis=-1)
        mixed_in = mixed_in.reshape(rows, D_MODEL).astype(BF16)
        xr = _gmm(mixed_in, [p['w_out']], grp, tm=tm, tn=512, out_dtype=F32,
                  gate=as_gate(g1), rows_per_gate=gate_rows, res=xr)

        if l % 2 == 0:
            h2 = _norm_mod(xr, p['norm2_g'][l], as_gate(sc2), as_gate(sh2), tr=tr, rows_per_gate=gate_rows)
            xr = _dense_ffn(h2, xr, as_gate(g2), gate_rows, p['ffn_w1'], p['ffn_w3'], p['ffn_w2'], l // 2, tm=tm)
        else:
            h2, logits = _norm_mod(xr, p['norm2_g'][l], as_gate(sc2), as_gate(sh2), tr=tr,
                                   rows_per_gate=gate_rows, router_w=p['moe_router_w'][l // 2])
            g2_rows = jnp.broadcast_to(g2, (b, t, D_MODEL)).reshape(rows, D_MODEL)
            xr = _moe_ffn(h2, logits, xr, g2_rows, p['moe_router_b'][l // 2],
                          p['moe_w1'], p['moe_w3'], p['moe_w2'], l // 2, tm=moe_tm)
        for lst, val in zip(new, (k_new, v_new, ssd_new, conv_new, pool_new, re_new, im_new)):
            lst.append(val)
    return xr.reshape(b, t, D_MODEL), [jnp.stack(v, axis=0) for v in new]


def kernel(x_prompt, x_sample, cache_k, cache_v, state_ssd, state_ssd_conv, state_pool,
           state_s5_re, state_s5_im, page_table, c_prompt, c_sample,
           ada_w, ada_b, norm1_g, norm2_g, w_in, w_out,
           ssd_conv_w, ssd_conv_b, ssd_dt_bias, ssd_a_log, ssd_d, ssd_norm_g,
           pool_w, pool_scale, pool_norm_g,
           s5_a_re, s5_a_im, s5_log_dt, s5_b_re, s5_b_im, s5_c_re, s5_c_im,
           s5_d, s5_glu_w, s5_glu_b, s5_norm_g,
           attn_q_g, attn_k_g, attn_norm_g,
           ffn_w1, ffn_w3, ffn_w2,
           moe_router_w, moe_router_b, moe_w1, moe_w3, moe_w2):
    w_in_b = w_in[:, :, OFF_POOL:]
    w_in_dt = jnp.pad(w_in[:, :, OFF_DT:OFF_POOL], ((0, 0), (0, 0), (0, LANES - SSD_HEADS)))
    p = dict(norm1_g=norm1_g, norm2_g=norm2_g, w_in=w_in, w_in_b=w_in_b, w_in_dt=w_in_dt, w_out=w_out,
             ssd_conv_w=ssd_conv_w, ssd_conv_b=ssd_conv_b, ssd_dt_bias=ssd_dt_bias,
             ssd_a_log=ssd_a_log, ssd_d=ssd_d, ssd_norm_g=ssd_norm_g,
             pool_w=pool_w, pool_scale=pool_scale, pool_norm_g=pool_norm_g,
             s5_a_re=s5_a_re, s5_a_im=s5_a_im, s5_log_dt=s5_log_dt, s5_b_re=s5_b_re, s5_b_im=s5_b_im,
             s5_c_re=s5_c_re, s5_c_im=s5_c_im, s5_d=s5_d, s5_glu_w=s5_glu_w, s5_glu_b=s5_glu_b,
             s5_norm_g=s5_norm_g, attn_q_g=attn_q_g, attn_k_g=attn_k_g, attn_norm_g=attn_norm_g,
             ffn_w1=ffn_w1, ffn_w3=ffn_w3, ffn_w2=ffn_w2,
             moe_router_w=moe_router_w, moe_router_b=moe_router_b,
             moe_w1=moe_w1, moe_w3=moe_w3, moe_w2=moe_w2)

    c_all = jnp.concatenate([c_prompt, c_sample], axis=0)
    n_c = c_all.shape[0]
    c_pad = jnp.pad(jax.nn.silu(c_all), ((0, BF16_SUBLANES - n_c), (0, 0))).astype(BF16)
    mods_p, mods_s = [], []
    for l in range(DEPTH):
        m = _gmm(c_pad, [ada_w], jnp.full((1,), l, jnp.int32), tm=BF16_SUBLANES, tn=512, out_dtype=F32)
        m = m[:n_c] + ada_b[l]
        mods_p.append(m[:BATCH])
        mods_s.append(m[BATCH:])

    y_prompt, st_p = _run_trunk(x_prompt, mods_p, 0, p, None, None)
    cache = (cache_k, cache_v, state_ssd, state_ssd_conv, state_pool, state_s5_re, state_s5_im)
    y_sample, st_s = _run_trunk(x_sample, mods_s, PAST_LEN, p, cache, page_table)
    k_p, v_p, ssd_p, conv_p, pool_p, re_p, im_p = st_p
    k_s, v_s, ssd_s, conv_s, pool_s, re_s, im_s = st_s
    return (y_prompt, y_sample, k_p, v_p, ssd_p, conv_p, pool_p, re_p, im_p,
            k_s, v_s, ssd_s, conv_s, pool_s, re_s, im_s)
```

```python
import functools
import math

import jax
import jax.numpy as jnp
from jax import lax
from jax.experimental import pallas as pl
from jax.experimental.pallas import tpu as pltpu

D_MODEL = 4096
BATCH = 4
SEQ = 2048
DEPTH = 4
DEC_BATCH = 8
DEC_SEQ = 4
PAST_LEN = 8192
PAGE_SIZE = 128
D_GROUP = D_MODEL // 4
SSD_HEADDIM = 64
SSD_HEADS = D_GROUP // SSD_HEADDIM
SSD_STATE = 128
SSD_GROUPS = 2
SSD_CONV = 4
SSD_CHUNK = 128
SSD_CONV_DIM = D_GROUP + 2 * SSD_GROUPS * SSD_STATE
POOL_WINDOWS = (2, 4, 8, 16)
POOL_GROUP = D_GROUP // len(POOL_WINDOWS)
POOL_BUF = max(POOL_WINDOWS) - 1
S5_CH = 16
S5_GROUPS = D_GROUP // S5_CH
S5_STATE = 64
ATTN_HEADDIM = 128
ATTN_HEADS = D_GROUP // ATTN_HEADDIM
MOBA_BLOCK = 256
MOBA_TOPK = 3
MOBA_QCHUNK = 16
D_FF = 11008
N_EXPERTS = 8
TOP_K = 2
D_FF_EXPERT = D_FF // 2
EPS = 1e-6
OFF_Z = 0
OFF_XBC = OFF_Z + D_GROUP
OFF_DT = OFF_XBC + SSD_CONV_DIM
OFF_POOL = OFF_DT + SSD_HEADS
OFF_S5 = OFF_POOL + D_GROUP
OFF_Q = OFF_S5 + D_GROUP
OFF_K = OFF_Q + D_GROUP
OFF_V = OFF_K + D_GROUP
D_IN = OFF_V + D_GROUP

F32 = jnp.float32
BF16 = jnp.bfloat16
LANES = 128
BF16_SUBLANES = 16
VMEM_LIMIT_BYTES = 56 * 1024 * 1024


def _gmm_kernel(*refs, n_w, has_res, has_gate, has_partial, has_rowscale, has_valid):
    it = iter(refs)
    te_ref, tv_ref, x_ref = next(it), next(it), next(it)
    w_refs = [next(it) for _ in range(n_w)]
    res_ref = next(it) if has_res else None
    gate_ref = next(it) if has_gate else None
    partial_ref = next(it) if has_partial else None
    rowscale_ref = next(it) if has_rowscale else None
    o_ref = next(it)
    wbf_refs = [next(it) for _ in range(n_w)]

    i = pl.program_id(1)
    prev = te_ref[jnp.maximum(i - 1, 0)]
    group_changed = jnp.logical_or(i == 0, te_ref[i] != prev)

    @pl.when(group_changed)
    def _():
        for w_ref, wbf_ref in zip(w_refs, wbf_refs):
            wbf_ref[...] = w_ref[0].astype(BF16)

    def compute():
        x = x_ref[...]
        acc = jnp.dot(x, wbf_refs[0][...], preferred_element_type=F32)
        if n_w == 2:
            up = jnp.dot(x, wbf_refs[1][...], preferred_element_type=F32)
            acc = (acc * (1.0 / (1.0 + jnp.exp(-acc)))) * up
        if has_partial:
            acc = partial_ref[...] + acc
        if has_rowscale:
            acc = rowscale_ref[...] * acc
        if has_gate:
            acc = gate_ref[0] * acc
        if has_res:
            acc = res_ref[...] + acc
        o_ref[...] = acc.astype(o_ref.dtype)

    if has_valid:
        pl.when(tv_ref[i] != 0)(compute)

        @pl.when(tv_ref[i] == 0)
        def _():
            o_ref[...] = jnp.zeros_like(o_ref)
    else:
        compute()


def _gmm(x, ws, tile_group, *, tm, tn, out_dtype, name, n_cols=None, x_col_block=0, k_dim=None,
         res=None, gate=None, rows_per_gate=None, partial=None, rowscale=None, tile_valid=None):
    p_rows = x.shape[0]
    k = ws[0].shape[1] if k_dim is None else k_dim
    n_cols = ws[0].shape[2] if n_cols is None else n_cols
    n_row_tiles = p_rows // tm
    assert n_row_tiles * tm == p_rows and tile_group.shape == (n_row_tiles,)
    has_valid = tile_valid is not None
    if tile_valid is None:
        tile_valid = jnp.ones((n_row_tiles,), jnp.int32)
    grid = (pl.cdiv(n_cols, tn), n_row_tiles)

    in_specs = [pl.BlockSpec((tm, k), lambda j, i, te, tv: (i, x_col_block))]
    args = [x]
    for w in ws:
        in_specs.append(pl.BlockSpec((1, k, tn), lambda j, i, te, tv: (te[i], 0, j)))
        args.append(w)
    if res is not None:
        in_specs.append(pl.BlockSpec((tm, tn), lambda j, i, te, tv: (i, j)))
        args.append(res)
    if gate is not None:
        rg = gate.shape[1]
        in_specs.append(pl.BlockSpec((1, rg, tn), lambda j, i, te, tv: ((i * tm) // rows_per_gate, 0, j)))
        args.append(gate)
    if partial is not None:
        in_specs.append(pl.BlockSpec((tm, tn), lambda j, i, te, tv: (i, j)))
        args.append(partial)
    if rowscale is not None:
        in_specs.append(pl.BlockSpec((tm, 1), lambda j, i, te, tv: (i, 0)))
        args.append(rowscale)

    kernel = functools.partial(
        _gmm_kernel, n_w=len(ws), has_res=res is not None, has_gate=gate is not None,
        has_partial=partial is not None, has_rowscale=rowscale is not None, has_valid=has_valid)
    return pl.pallas_call(
        kernel,
        name=name,
        grid_spec=pltpu.PrefetchScalarGridSpec(
            num_scalar_prefetch=2,
            grid=grid,
            in_specs=in_specs,
            out_specs=pl.BlockSpec((tm, tn), lambda j, i, te, tv: (i, j)),
            scratch_shapes=[pltpu.VMEM((k, tn), BF16) for _ in ws],
        ),
        out_shape=jax.ShapeDtypeStruct((p_rows, n_cols), out_dtype),
        compiler_params=pltpu.CompilerParams(
            dimension_semantics=("arbitrary", "arbitrary"),
            vmem_limit_bytes=VMEM_LIMIT_BYTES),
    )(tile_group, tile_valid, *args)


def _norm_mod_kernel(*refs, has_router):
    if has_router:
        x_ref, g_ref, sc_ref, sh_ref, rw_ref, h_ref, logit_ref = refs
    else:
        x_ref, g_ref, sc_ref, sh_ref, h_ref = refs
    xf = x_ref[...]
    ms = jnp.mean(xf * xf, axis=-1, keepdims=True)
    y = xf * lax.rsqrt(ms + EPS) * g_ref[...]
    h = y * (1.0 + sc_ref[0]) + sh_ref[0]
    h_ref[...] = h.astype(BF16)
    if has_router:
        logit_ref[...] = jnp.dot(h, rw_ref[...], preferred_element_type=F32,
                                 precision=lax.Precision.HIGHEST)


def _norm_mod(x, g, sc, sh, *, tr, rows_per_gate, router_w=None):
    r, d = x.shape
    rg = sc.shape[1]
    mod_spec = pl.BlockSpec((1, rg, d), lambda i: ((i * tr) // rows_per_gate, 0, 0))
    in_specs = [pl.BlockSpec((tr, d), lambda i: (i, 0)), pl.BlockSpec((1, d), lambda i: (0, 0)),
                mod_spec, mod_spec]
    args = [x, g.reshape(1, d), sc, sh]
    out_shape = [jax.ShapeDtypeStruct((r, d), BF16)]
    out_specs = [pl.BlockSpec((tr, d), lambda i: (i, 0))]
    if router_w is not None:
        rw = jnp.pad(router_w, ((0, 0), (0, LANES - router_w.shape[1])))
        in_specs.append(pl.BlockSpec((d, LANES), lambda i: (0, 0)))
        args.append(rw)
        out_shape.append(jax.ShapeDtypeStruct((r, LANES), F32))
        out_specs.append(pl.BlockSpec((tr, LANES), lambda i: (i, 0)))
    out = pl.pallas_call(
        functools.partial(_norm_mod_kernel, has_router=router_w is not None),
        name="norm_mod",
        grid=(r // tr,),
        in_specs=in_specs,
        out_specs=out_specs,
        out_shape=out_shape,
        compiler_params=pltpu.CompilerParams(
            dimension_semantics=("arbitrary",), vmem_limit_bytes=VMEM_LIMIT_BYTES),
    )(*args)
    return out if router_w is not None else out[0]


def _qk_norm_kernel(q_ref, k_ref, v_ref, qg_ref, kg_ref, qn_ref, kn_ref, knb_ref, vb_ref, km_ref):
    for hd in range(ATTN_HEADS):
        sl = slice(hd * ATTN_HEADDIM, (hd + 1) * ATTN_HEADDIM)
        q = q_ref[0, :, sl]
        k = k_ref[0, :, sl]
        qn = q * lax.rsqrt(jnp.mean(q * q, axis=-1, keepdims=True) + EPS) * qg_ref[...]
        kn = k * lax.rsqrt(jnp.mean(k * k, axis=-1, keepdims=True) + EPS) * kg_ref[...]
        qn_ref[0, :, sl] = qn
        kn_ref[0, :, sl] = kn
        knb_ref[0, :, sl] = kn.astype(BF16)
        km_ref[0, :, sl] = jnp.mean(kn, axis=0, keepdims=True)
    vb_ref[0] = v_ref[0].astype(BF16)


def _moba_kernel(slopes_ref, q_ref, k_ref, v_ref, km_ref, o_ref):
    hd = pl.program_id(1)
    n = pl.program_id(2)
    blk_rows = MOBA_BLOCK
    q = q_ref[0]
    km = km_ref[0]
    gate = lax.dot_general(q, km, (((1,), (1,)), ((), ())), preferred_element_type=F32,
                           precision=lax.Precision.HIGHEST)
    blk_id = lax.broadcasted_iota(jnp.int32, gate.shape, 1)
    past = blk_id < n
    gm = jnp.where(past, gate, -jnp.inf)
    rank = jnp.zeros(gate.shape, jnp.int32)
    for m in range(gate.shape[1]):
        col = gm[:, m:m + 1]
        beats = jnp.logical_or(col > gm, jnp.logical_and(col == gm, m < blk_id))
        rank = rank + beats.astype(jnp.int32)
    sel = jnp.logical_and(past, rank < MOBA_TOPK).astype(F32)

    qb = q.astype(BF16)
    slope = slopes_ref[hd]
    scale = 1.0 / math.sqrt(ATTN_HEADDIM)
    r_id = lax.broadcasted_iota(jnp.int32, (blk_rows, blk_rows), 0)
    c_id = lax.broadcasted_iota(jnp.int32, (blk_rows, blk_rows), 1)
    rc = (r_id - c_id).astype(F32)

    def scores(j):
        kj = k_ref[0, pl.ds(pl.multiple_of(j * blk_rows, blk_rows), blk_rows), :]
        return lax.dot_general(qb, kj, (((1,), (1,)), ((), ())), preferred_element_type=F32) * scale

    def values(j):
        return v_ref[0, pl.ds(pl.multiple_of(j * blk_rows, blk_rows), blk_rows), :]

    s = scores(n) - slope * rc
    s = jnp.where(c_id <= r_id, s, -jnp.inf)
    m0 = jnp.max(s, axis=-1, keepdims=True)
    p = jnp.exp(s - m0)
    l0 = jnp.sum(p, axis=-1, keepdims=True)
    acc0 = jnp.dot(p.astype(BF16), values(n), preferred_element_type=F32)

    def body(j, carry):
        m_run, l_run, acc = carry
        dist = rc + ((n - j) * blk_rows).astype(F32)
        sj = scores(j) - slope * dist
        sel_j = jnp.sum(jnp.where(blk_id == j, sel, 0.0), axis=-1, keepdims=True) > 0.5
        sj = jnp.where(sel_j, sj, -jnp.inf)
        m_new = jnp.maximum(m_run, jnp.max(sj, axis=-1, keepdims=True))
        alpha = jnp.exp(m_run - m_new)
        pj = jnp.exp(sj - m_new)
        l_new = alpha * l_run + jnp.sum(pj, axis=-1, keepdims=True)
        acc = alpha * acc + jnp.dot(pj.astype(BF16), values(j), preferred_element_type=F32)
        return m_new, l_new, acc

    _, l_fin, acc = lax.fori_loop(0, n, body, (m0, l0, acc0))
    o_ref[0] = acc / l_fin


def _moba_prompt(proj, q_g, k_g, *, col0):
    b, t, _ = proj.shape
    nblk = t // MOBA_BLOCK
    assert nblk * MOBA_BLOCK == t
    row_spec = lambda c: pl.BlockSpec((1, MOBA_BLOCK, D_GROUP), lambda bi, ni, c=c: (bi, ni, c))
    g_spec = pl.BlockSpec((1, ATTN_HEADDIM), lambda bi, ni: (0, 0))
    out_spec = pl.BlockSpec((1, MOBA_BLOCK, D_GROUP), lambda bi, ni: (bi, ni, 0))
    qn, kn, knb, vb, km = pl.pallas_call(
        _qk_norm_kernel,
        name="qk_norm",
        grid=(b, nblk),
        in_specs=[row_spec(col0), row_spec(col0 + 1), row_spec(col0 + 2), g_spec, g_spec],
        out_specs=[out_spec, out_spec, out_spec, out_spec,
                   pl.BlockSpec((1, 1, D_GROUP), lambda bi, ni: (bi * nblk + ni, 0, 0))],
        out_shape=[jax.ShapeDtypeStruct((b, t, D_GROUP), F32), jax.ShapeDtypeStruct((b, t, D_GROUP), F32),
                   jax.ShapeDtypeStruct((b, t, D_GROUP), BF16), jax.ShapeDtypeStruct((b, t, D_GROUP), BF16),
                   jax.ShapeDtypeStruct((b * nblk, 1, D_GROUP), F32)],
        compiler_params=pltpu.CompilerParams(
            dimension_semantics=("arbitrary", "arbitrary"), vmem_limit_bytes=VMEM_LIMIT_BYTES),
    )(proj, proj, proj, q_g.reshape(1, ATTN_HEADDIM), k_g.reshape(1, ATTN_HEADDIM))
    km = km.reshape(b, nblk, D_GROUP)

    slopes = _alibi_slopes(ATTN_HEADS)
    o = pl.pallas_call(
        _moba_kernel,
        name="moba_attend",
        grid_spec=pltpu.PrefetchScalarGridSpec(
            num_scalar_prefetch=1,
            grid=(b, ATTN_HEADS, nblk),
            in_specs=[
                pl.BlockSpec((1, MOBA_BLOCK, ATTN_HEADDIM), lambda bi, hi, ni, sl: (bi, ni, hi)),
                pl.BlockSpec((1, t, ATTN_HEADDIM), lambda bi, hi, ni, sl: (bi, 0, hi)),
                pl.BlockSpec((1, t, ATTN_HEADDIM), lambda bi, hi, ni, sl: (bi, 0, hi)),
                pl.BlockSpec((1, nblk, ATTN_HEADDIM), lambda bi, hi, ni, sl: (bi, 0, hi)),
            ],
            out_specs=pl.BlockSpec((1, MOBA_BLOCK, ATTN_HEADDIM), lambda bi, hi, ni, sl: (bi, ni, hi)),
        ),
        out_shape=jax.ShapeDtypeStruct((b, t, D_GROUP), F32),
        compiler_params=pltpu.CompilerParams(
            dimension_semantics=("arbitrary", "arbitrary", "arbitrary"), vmem_limit_bytes=VMEM_LIMIT_BYTES),
    )(slopes, qn, knb, vb, km)
    return o, kn


S5_LANE_BLOCK = 512
S5_CH_BLOCK = S5_LANE_BLOCK // S5_STATE * S5_CH
S5_N_BLOCKS = D_GROUP // S5_CH_BLOCK
S5_ROWS = 8
S5_MM_ROWS = 256


def _gelu_tanh(x):
    return 0.5 * x * (1.0 + jnp.tanh(math.sqrt(2.0 / math.pi) * (x + 0.044715 * (x * x * x))))


def _s5_kernel(u_ref, win_ref, tab_ref, wc_ref, d_ref, y_ref, hre_ref, him_ref, bre, bim, *, t_len):
    nl = S5_LANE_BLOCK

    def in_mm(c, carry):
        rows = pl.ds(pl.multiple_of(c * S5_MM_ROWS, S5_MM_ROWS), S5_MM_ROWS)
        r = jnp.dot(u_ref[0, rows, :].astype(BF16), win_ref[0], preferred_element_type=F32)
        bre[rows, :] = r[:, :nl]
        bim[rows, :] = r[:, nl:]
        return carry

    lax.fori_loop(0, t_len // S5_MM_ROWS, in_mm, 0)

    pre = tab_ref[0, 0:8, :]
    pim = tab_ref[0, 8:16, :]
    steps = [(1, tab_ref[0, 16:17, :], tab_ref[0, 17:18, :]),
             (2, tab_ref[0, 18:19, :], tab_ref[0, 19:20, :]),
             (4, tab_ref[0, 20:21, :], tab_ref[0, 21:22, :])]
    row = lax.broadcasted_iota(jnp.int32, (S5_ROWS, nl), 0)

    def scan_rows(i, carry):
        cr, ci = carry
        rows = pl.ds(pl.multiple_of(i * S5_ROWS, S5_ROWS), S5_ROWS)
        xr = bre[rows, :]
        xi = bim[rows, :]
        for s, ar, ai in steps:
            sr = jnp.where(row >= s, pltpu.roll(xr, s, 0), 0.0)
            si = jnp.where(row >= s, pltpu.roll(xi, s, 0), 0.0)
            xr, xi = xr + (ar * sr - ai * si), xi + (ar * si + ai * sr)
        xr, xi = xr + (pre * cr - pim * ci), xi + (pre * ci + pim * cr)
        bre[rows, :] = xr
        bim[rows, :] = xi
        return (jnp.broadcast_to(xr[S5_ROWS - 1:S5_ROWS, :], (S5_ROWS, nl)),
                jnp.broadcast_to(xi[S5_ROWS - 1:S5_ROWS, :], (S5_ROWS, nl)))

    zero = jnp.zeros((S5_ROWS, nl), F32)
    cr, ci = lax.fori_loop(0, t_len // S5_ROWS, scan_rows, (zero, zero), unroll=2)
    hre_ref[0] = cr[0:1, :]
    him_ref[0] = ci[0:1, :]

    def out_mm(c, carry):
        rows = pl.ds(pl.multiple_of(c * S5_MM_ROWS, S5_MM_ROWS), S5_MM_ROWS)
        hcat = jnp.concatenate([bre[rows, :], bim[rows, :]], axis=1).astype(BF16)
        y = jnp.dot(hcat, wc_ref[0], preferred_element_type=F32)
        y = y + d_ref[...] * u_ref[0, rows, :]
        y_ref[0, rows, :] = _gelu_tanh(y)
        return carry

    lax.fori_loop(0, t_len // S5_MM_ROWS, out_mm, 0)


def _s5_tables(a_re, a_im, log_dt, b_re, b_im, c_re, c_im):
    dt = jnp.exp(log_dt.astype(F32))[:, None]
    ar, ai = a_re.astype(F32), a_im.astype(F32)
    mag = jnp.exp(dt * ar)
    abar_re, abar_im = mag * jnp.cos(dt * ai), mag * jnp.sin(dt * ai)
    den = ar * ar + ai * ai
    nr, ni = abar_re - 1.0, abar_im
    f_re, f_im = (nr * ar + ni * ai) / den, (ni * ar - nr * ai) / den
    bb_re = f_re[..., None] * b_re - f_im[..., None] * b_im
    bb_im = f_re[..., None] * b_im + f_im[..., None] * b_re

    def cmul(x, y):
        return x[0] * y[0] - x[1] * y[1], x[0] * y[1] + x[1] * y[0]

    a1 = (abar_re.reshape(-1), abar_im.reshape(-1))
    pows = [a1]
    for _ in range(S5_ROWS - 1):
        pows.append(cmul(pows[-1], a1))
    a2 = pows[1]
    a4 = pows[3]
    rows = ([p[0] for p in pows] + [p[1] for p in pows] +
            [a1[0], a1[1], a2[0], a2[1], a4[0], a4[1]] + [jnp.zeros_like(a1[0])] * 2)
    tab = jnp.stack(rows, axis=0)
    tab = tab.reshape(24, S5_N_BLOCKS, S5_LANE_BLOCK).transpose(1, 0, 2)

    gpb = S5_LANE_BLOCK // S5_STATE
    eye = jnp.eye(gpb, dtype=F32)

    def pack_in(bb):
        x = bb.reshape(S5_N_BLOCKS, gpb, S5_STATE, S5_CH)
        x = jnp.einsum('kgnc,gh->kgchn', x, eye)
        return x.reshape(S5_N_BLOCKS, gpb * S5_CH, gpb * S5_STATE)

    def pack_out(cc):
        x = cc.reshape(S5_N_BLOCKS, gpb, S5_CH, S5_STATE)
        x = jnp.einsum('kgcn,gh->kgnhc', x, eye)
        return x.reshape(S5_N_BLOCKS, gpb * S5_STATE, gpb * S5_CH)

    w_in = jnp.concatenate([pack_in(bb_re), pack_in(bb_im)], axis=2).astype(BF16)
    w_c = jnp.concatenate([pack_out(c_re.astype(F32)), -pack_out(c_im.astype(F32))], axis=1).astype(BF16)
    return tab, w_in, w_c


def _s5_post_kernel(y_ref, w_ref, b_ref, g_ref, o_ref, wbf_ref):
    @pl.when(pl.program_id(0) == 0)
    def _():
        wbf_ref[...] = w_ref[...].astype(BF16)

    y = y_ref[...]
    z = jnp.dot(y.astype(BF16), wbf_ref[...], preferred_element_type=F32) + b_ref[...]
    y = y * (1.0 / (1.0 + jnp.exp(-z)))
    y = y * lax.rsqrt(jnp.mean(y * y, axis=-1, keepdims=True) + EPS) * g_ref[...]
    o_ref[...] = y.astype(o_ref.dtype)


def _s5_prompt(proj, p, l, *, col0):
    b, t, _ = proj.shape
    tab, w_in, w_c = _s5_tables(p['s5_a_re'][l], p['s5_a_im'][l], p['s5_log_dt'][l],
                                p['s5_b_re'][l], p['s5_b_im'][l], p['s5_c_re'][l], p['s5_c_im'][l])
    nl = S5_LANE_BLOCK
    y, hre, him = pl.pallas_call(
        functools.partial(_s5_kernel, t_len=t),
        name="s5_scan",
        grid=(b, S5_N_BLOCKS),
        in_specs=[
            pl.BlockSpec((1, t, S5_CH_BLOCK), lambda bi, ki: (bi, 0, col0 + ki)),
            pl.BlockSpec((1, S5_CH_BLOCK, 2 * nl), lambda bi, ki: (ki, 0, 0)),
            pl.BlockSpec((1, 24, nl), lambda bi, ki: (ki, 0, 0)),
            pl.BlockSpec((1, 2 * nl, S5_CH_BLOCK), lambda bi, ki: (ki, 0, 0)),
            pl.BlockSpec((1, S5_CH_BLOCK), lambda bi, ki: (0, ki)),
        ],
        out_specs=[
            pl.BlockSpec((1, t, S5_CH_BLOCK), lambda bi, ki: (bi, 0, ki)),
            pl.BlockSpec((1, 1, nl), lambda bi, ki: (bi, 0, ki)),
            pl.BlockSpec((1, 1, nl), lambda bi, ki: (bi, 0, ki)),
        ],
        out_shape=[jax.ShapeDtypeStruct((b, t, D_GROUP), F32),
                   jax.ShapeDtypeStruct((b, 1, S5_GROUPS * S5_STATE), F32),
                   jax.ShapeDtypeStruct((b, 1, S5_GROUPS * S5_STATE), F32)],
        scratch_shapes=[pltpu.VMEM((t, nl), F32), pltpu.VMEM((t, nl), F32)],
        compiler_params=pltpu.CompilerParams(
            dimension_semantics=("arbitrary", "arbitrary"), vmem_limit_bytes=VMEM_LIMIT_BYTES),
    )(proj, w_in, tab, w_c, p['s5_d'][l].reshape(1, D_GROUP))

    rows = b * t
    tr = 256
    yn = pl.pallas_call(
        _s5_post_kernel,
        name="s5_post",
        grid=(rows // tr,),
        in_specs=[pl.BlockSpec((tr, D_GROUP), lambda i: (i, 0)),
                  pl.BlockSpec((D_GROUP, D_GROUP), lambda i: (0, 0)),
                  pl.BlockSpec((1, D_GROUP), lambda i: (0, 0)),
                  pl.BlockSpec((1, D_GROUP), lambda i: (0, 0))],
        out_specs=pl.BlockSpec((tr, D_GROUP), lambda i: (i, 0)),
        out_shape=jax.ShapeDtypeStruct((rows, D_GROUP), BF16),
        scratch_shapes=[pltpu.VMEM((D_GROUP, D_GROUP), BF16)],
        compiler_params=pltpu.CompilerParams(
            dimension_semantics=("arbitrary",), vmem_limit_bytes=VMEM_LIMIT_BYTES),
    )(y.reshape(rows, D_GROUP), p['s5_glu_w'][l], p['s5_glu_b'][l].reshape(1, D_GROUP),
      p['s5_norm_g'][l].reshape(1, D_GROUP))
    return yn, hre.reshape(b, S5_GROUPS, S5_STATE), him.reshape(b, S5_GROUPS, S5_STATE)


def _rmsnorm(x, g):
    xf = x.astype(F32)
    xf = xf * lax.rsqrt(jnp.mean(xf * xf, axis=-1, keepdims=True) + EPS)
    return (xf * g.astype(F32)).astype(x.dtype)


def _alibi_slopes(n):
    return 2.0 ** (-8.0 * jnp.arange(1, n + 1, dtype=F32) / n)


def _causal_depthwise_conv(ext, w, bias, t):
    out = bias
    for k in range(w.shape[0]):
        out = out + ext[:, k:k + t] * w[k]
    return out


def _ssd_chunked_scan(x, dt, a, bm, cm, h0, chunk):
    b, t, h, p = x.shape
    nc = t // chunk
    rep = h // SSD_GROUPS
    bh = jnp.repeat(bm, rep, axis=2).astype(F32)
    ch = jnp.repeat(cm, rep, axis=2).astype(F32)
    da = dt * a
    resh = lambda v: v.reshape((b, nc, chunk) + v.shape[2:])
    xc, dtc, ac, bc, cc = resh(x.astype(F32)), resh(dt), resh(da), resh(bh), resh(ch)
    acum = jnp.cumsum(ac, axis=2)
    diff = acum[:, :, :, None, :] - acum[:, :, None, :, :]
    causal = jnp.tril(jnp.ones((chunk, chunk), dtype=bool))
    decay = jnp.exp(jnp.where(causal[None, None, :, :, None], diff, -jnp.inf))
    scores = jnp.einsum('bcihn,bcjhn->bcijh', cc, bc)
    y_diag = jnp.einsum('bcijh,bcjhp->bcihp', scores * decay * dtc[:, :, None], xc)
    decay_end = jnp.exp(acum[:, :, -1:, :] - acum)
    states = jnp.einsum('bcjhn,bcjh,bcjhp->bchpn', bc, decay_end * dtc, xc)
    chunk_decay = jnp.exp(acum[:, :, -1, :])

    def step(hh, inp):
        dec, st = inp
        return hh * dec[:, :, None, None] + st, hh

    h_last, h_start = lax.scan(step, h0.astype(F32),
                               (jnp.moveaxis(chunk_decay, 1, 0), jnp.moveaxis(states, 1, 0)))
    h_start = jnp.moveaxis(h_start, 0, 1)
    y_off = jnp.einsum('bcihn,bchpn->bcihp', cc, h_start) * jnp.exp(acum)[..., None]
    y = (y_diag + y_off).reshape(b, t, h, p)
    return y.astype(x.dtype), h_last.astype(h0.dtype)


def _ssd_mixer(z, xbc, dt_raw, conv_buf, h0, conv_w, conv_b, dt_bias, a_log, d_skip, norm_g):
    b, t, _ = xbc.shape
    ext = jnp.concatenate([conv_buf.astype(xbc.dtype), xbc], axis=1)
    new_buf = ext[:, t:]
    xbc_c = jax.nn.silu(_causal_depthwise_conv(ext, conv_w, conv_b, t))
    gn = SSD_GROUPS * SSD_STATE
    xs = xbc_c[..., :D_GROUP].reshape(b, t, SSD_HEADS, SSD_HEADDIM)
    bm = xbc_c[..., D_GROUP:D_GROUP + gn].reshape(b, t, SSD_GROUPS, SSD_STATE)
    cm = xbc_c[..., D_GROUP + gn:].reshape(b, t, SSD_GROUPS, SSD_STATE)
    dt = jax.nn.softplus((dt_raw + dt_bias).astype(F32))
    a = -jnp.exp(a_log.astype(F32))
    chunk = SSD_CHUNK if t % SSD_CHUNK == 0 else t
    y, h_last = _ssd_chunked_scan(xs, dt, a, bm, cm, h0, chunk)
    y = (y + xs * d_skip[:, None]).reshape(b, t, D_GROUP)
    y = _rmsnorm(y * jax.nn.silu(z), norm_g)
    return y, new_buf, h_last


def _pool_mixer(u, buf, pos0, pool_w, pool_scale):
    b, t, c = u.shape
    ext = jnp.concatenate([buf.astype(u.dtype), u], axis=1)
    new_buf = ext[:, t:]
    cs = jnp.cumsum(ext.astype(F32), axis=1)
    cs = jnp.concatenate([jnp.zeros_like(cs[:, :1]), cs], axis=1)
    pos = pos0 + jnp.arange(t, dtype=jnp.int32)
    outs = []
    for g, w in enumerate(POOL_WINDOWS):
        sl = slice(g * POOL_GROUP, (g + 1) * POOL_GROUP)
        end = cs[:, POOL_BUF + 1:POOL_BUF + 1 + t, sl]
        start = cs[:, POOL_BUF + 1 - w:POOL_BUF + 1 - w + t, sl]
        cnt = jnp.minimum(w, pos + 1).astype(F32)[None, :, None]
        d = ((end - start) / cnt - u[..., sl].astype(F32)).astype(u.dtype)
        outs.append(d @ pool_w[g])
    return jnp.concatenate(outs, axis=-1) * pool_scale, new_buf


def _s5_mixer(u, h0_re, h0_im, a_re, a_im, log_dt, b_re, b_im, c_re, c_im, d_skip, glu_w, glu_b):
    b, t, c = u.shape
    ug = u.reshape(b, t, S5_GROUPS, S5_CH).astype(F32)
    dt = jnp.exp(log_dt.astype(F32))[:, None]
    ar, ai = a_re.astype(F32), a_im.astype(F32)
    mag = jnp.exp(dt * ar)
    abar_re, abar_im = mag * jnp.cos(dt * ai), mag * jnp.sin(dt * ai)
    den = ar * ar + ai * ai
    nr, ni = abar_re - 1.0, abar_im
    f_re, f_im = (nr * ar + ni * ai) / den, (ni * ar - nr * ai) / den
    bb_re = f_re[..., None] * b_re - f_im[..., None] * b_im
    bb_im = f_re[..., None] * b_im + f_im[..., None] * b_re
    bu_re = jnp.einsum('gnc,btgc->btgn', bb_re, ug)
    bu_im = jnp.einsum('gnc,btgc->btgn', bb_im, ug)
    a_full_re = jnp.broadcast_to(abar_re, bu_re.shape)
    a_full_im = jnp.broadcast_to(abar_im, bu_im.shape)

    def combine(e1, e2):
        a1r, a1i, b1r, b1i = e1
        a2r, a2i, b2r, b2i = e2
        return (a2r * a1r - a2i * a1i, a2r * a1i + a2i * a1r,
                a2r * b1r - a2i * b1i + b2r, a2r * b1i + a2i * b1r + b2i)

    _, _, hr, hi = lax.associative_scan(combine, (a_full_re, a_full_im, bu_re, bu_im), axis=1)
    k = jnp.arange(1, t + 1, dtype=F32)[:, None, None]
    pm = jnp.exp(k * dt * ar)
    pw_re, pw_im = pm * jnp.cos(k * dt * ai), pm * jnp.sin(k * dt * ai)
    h0r, h0i = h0_re.astype(F32)[:, None], h0_im.astype(F32)[:, None]
    hr = hr + pw_re * h0r - pw_im * h0i
    hi = hi + pw_re * h0i + pw_im * h0r
    y = jnp.einsum('gcn,btgn->btgc', c_re, hr) - jnp.einsum('gcn,btgn->btgc', c_im, hi)
    y = y.reshape(b, t, c) + d_skip * u.astype(F32)
    y = jax.nn.gelu(y)
    y = y * jax.nn.sigmoid(y @ glu_w.astype(F32) + glu_b)
    return y.astype(u.dtype), hr[:, -1].astype(h0_re.dtype), hi[:, -1].astype(h0_im.dtype)


def _moba_attend(q, k_all, v_all, q_pos):
    b, tq, h, dh = q.shape
    l = k_all.shape[1]
    nb = -(-l // MOBA_BLOCK)
    pad = nb * MOBA_BLOCK - l
    padw = ((0, 0), (0, pad), (0, 0), (0, 0))
    kb = jnp.pad(k_all, padw).reshape(b, nb, MOBA_BLOCK, h, dh).transpose(0, 3, 1, 2, 4)
    vb = jnp.pad(v_all, padw).reshape(b, nb, MOBA_BLOCK, h, dh).transpose(0, 3, 1, 2, 4)
    kmean = jnp.mean(kb.astype(F32), axis=3)
    slopes = _alibi_slopes(h)
    scale = 1.0 / math.sqrt(dh)
    bi = jnp.arange(b)[:, None, None, None]
    hi = jnp.arange(h)[None, :, None, None]
    blk_ids = jnp.arange(nb, dtype=jnp.int32)
    offs = jnp.arange(MOBA_BLOCK, dtype=jnp.int32)

    def attend_chunk(args):
        qc, pc = args
        n_q = pc.shape[0]
        own = pc // MOBA_BLOCK
        gate = jnp.einsum('bhqd,bhnd->bhqn', qc.astype(F32), kmean)
        gate = jnp.where(blk_ids[None, :] < own[:, None], gate, -jnp.inf)
        if nb < MOBA_TOPK:
            gate = jnp.pad(gate, ((0, 0), (0, 0), (0, 0), (0, MOBA_TOPK - nb)), constant_values=-jnp.inf)
        g_val, g_idx = lax.top_k(gate, MOBA_TOPK)
        sel = jnp.concatenate([jnp.minimum(g_idx, nb - 1),
                               jnp.broadcast_to(own[None, None, :, None], (b, h, n_q, 1))], axis=-1)
        sel_ok = jnp.concatenate([jnp.isfinite(g_val), jnp.ones((b, h, n_q, 1), dtype=bool)], axis=-1)
        kg = kb[bi, hi, sel]
        vg = vb[bi, hi, sel]
        s = jnp.einsum('bhqd,bhqsnd->bhqsn', qc, kg).astype(F32) * scale
        kpos = sel[..., None] * MOBA_BLOCK + offs
        qpos = pc[None, None, :, None, None]
        s = s - slopes[None, :, None, None, None] * (qpos - kpos).astype(F32)
        s = jnp.where(sel_ok[..., None] & (kpos <= qpos), s, -jnp.inf)
        p = jax.nn.softmax(s.reshape(b, h, n_q, -1), axis=-1).reshape(s.shape)
        return jnp.einsum('bhqsn,bhqsnd->bhqd', p.astype(vg.dtype), vg)

    qlen = MOBA_QCHUNK if tq % MOBA_QCHUNK == 0 else tq
    nq = tq // qlen
    qs = jnp.moveaxis(q.transpose(0, 2, 1, 3).reshape(b, h, nq, qlen, dh), 2, 0)
    ps = q_pos.reshape(nq, qlen)
    o = lax.map(attend_chunk, (qs, ps))
    return jnp.moveaxis(o, 0, 2).reshape(b, h, tq, dh).transpose(0, 2, 1, 3)


def _moba_mixer(q, k, v, k_past, v_past, pos0, q_g, k_g):
    b, t, _ = q.shape
    shp = (b, t, ATTN_HEADS, ATTN_HEADDIM)
    q = _rmsnorm(q.reshape(shp), q_g)
    k = _rmsnorm(k.reshape(shp), k_g)
    v = v.reshape(shp)
    if k_past is None:
        k_all, v_all = k, v
    else:
        k_all = jnp.concatenate([k_past.astype(k.dtype), k], axis=1)
        v_all = jnp.concatenate([v_past.astype(v.dtype), v], axis=1)
    o = _moba_attend(q, k_all, v_all, pos0 + jnp.arange(t, dtype=jnp.int32))
    return o.reshape(b, t, D_GROUP), k, v


def _dense_ffn(h2, x_res, g2, rows_per_gate, w1, w3, w2, l2, *, tm):
    rows = h2.shape[0]
    n_tiles = rows // tm
    grp = jnp.full((n_tiles,), l2, jnp.int32)
    act = _gmm(h2, [w1, w3], grp, tm=tm, tn=256, out_dtype=BF16, name="ffn_up")
    w2h = w2.reshape(w2.shape[0] * 2, D_FF_EXPERT, D_MODEL)
    part = _gmm(act, [w2h], grp * 2, tm=tm, tn=512, out_dtype=F32, x_col_block=0, k_dim=D_FF_EXPERT,
                name="ffn_down_lo")
    return _gmm(act, [w2h], grp * 2 + 1, tm=tm, tn=512, out_dtype=F32, x_col_block=1, k_dim=D_FF_EXPERT,
                name="ffn_down_hi", partial=part, gate=g2, rows_per_gate=rows_per_gate, res=x_res)


def _moe_ffn(h2, logits, x_res, g2_rows, router_b, w1, w3, w2, l2, *, tm):
    n_tok = h2.shape[0]
    lg = logits[:, :N_EXPERTS] + router_b.astype(F32)
    top_v, top_i = lax.top_k(lg, TOP_K)
    gates = jax.nn.softmax(top_v, axis=-1)

    n_pairs = n_tok * TOP_K
    n_tiles = -(-n_pairs // tm) + N_EXPERTS
    p_rows = n_tiles * tm
    e_flat = top_i.reshape(-1).astype(jnp.int32)
    onehot = (e_flat[:, None] == jnp.arange(N_EXPERTS, dtype=jnp.int32)[None, :]).astype(jnp.int32)
    rank = jnp.sum((jnp.cumsum(onehot, axis=0) - onehot) * onehot, axis=1)
    counts = jnp.sum(onehot, axis=0)
    tiles_e = (counts + tm - 1) // tm
    tile_end = jnp.cumsum(tiles_e)
    starts = (tile_end - tiles_e) * tm
    pos = starts[e_flat] + rank
    tok = jnp.arange(n_pairs, dtype=jnp.int32) // TOP_K
    row_token = jnp.zeros((p_rows,), jnp.int32).at[pos].set(tok)
    row_gate = jnp.zeros((p_rows,), F32).at[pos].set(gates.reshape(-1))
    tile_ids = jnp.arange(n_tiles, dtype=jnp.int32)
    total_tiles = tile_end[-1]
    tile_valid = (tile_ids < total_tiles).astype(jnp.int32)
    clipped = jnp.minimum(tile_ids, total_tiles - 1)
    tile_expert = jnp.sum((clipped[:, None] >= tile_end[None, :]).astype(jnp.int32), axis=1)
    tile_group = (l2 * N_EXPERTS + tile_expert).astype(jnp.int32)

    xs = jnp.take(h2, row_token, axis=0)
    n_moe = w1.shape[0]
    w1r = w1.reshape(n_moe * N_EXPERTS, D_MODEL, D_FF_EXPERT)
    w3r = w3.reshape(n_moe * N_EXPERTS, D_MODEL, D_FF_EXPERT)
    w2r = w2.reshape(n_moe * N_EXPERTS, D_FF_EXPERT, D_MODEL)
    act = _gmm(xs, [w1r, w3r], tile_group, tm=tm, tn=256, out_dtype=BF16, name="moe_up", tile_valid=tile_valid)
    ys = _gmm(act, [w2r], tile_group, tm=tm, tn=512, out_dtype=F32, name="moe_down",
              rowscale=row_gate.reshape(p_rows, 1), tile_valid=tile_valid)
    pos2 = pos.reshape(n_tok, TOP_K)
    f = jnp.take(ys, pos2[:, 0], axis=0) + jnp.take(ys, pos2[:, 1], axis=0)
    return x_res + g2_rows * f


def _run_trunk(x, mods, pos0, p, cache, page_table):
    b, t, _ = x.shape
    rows = b * t
    tm = 512 if rows % 512 == 0 else rows
    tr = 256 if rows % 256 == 0 else rows
    moe_tm = 512 if rows >= 4096 else BF16_SUBLANES
    n_tiles = rows // tm
    new = [[] for _ in range(7)]
    xr = x.reshape(rows, D_MODEL)
    for l in range(DEPTH):
        if cache is None:
            k_past = v_past = None
            ssd_h0 = jnp.zeros((b, SSD_HEADS, SSD_HEADDIM, SSD_STATE), x.dtype)
            conv_buf = jnp.zeros((b, SSD_CONV - 1, SSD_CONV_DIM), x.dtype)
            pool_buf = jnp.zeros((b, POOL_BUF, D_GROUP), x.dtype)
            s5_h0_re = jnp.zeros((b, S5_GROUPS, S5_STATE), x.dtype)
            s5_h0_im = jnp.zeros((b, S5_GROUPS, S5_STATE), x.dtype)
        else:
            cache_k, cache_v, st_ssd, st_conv, st_pool, st_re, st_im = cache
            k_past = cache_k[l][page_table].reshape(b, -1, ATTN_HEADS, ATTN_HEADDIM)
            v_past = cache_v[l][page_table].reshape(b, -1, ATTN_HEADS, ATTN_HEADDIM)
            ssd_h0, conv_buf, pool_buf = st_ssd[l], st_conv[l], st_pool[l]
            s5_h0_re, s5_h0_im = st_re[l], st_im[l]
        mod = mods[l]
        sh1, sc1, g1, sh2, sc2, g2 = [m.reshape(b, 1, D_MODEL) for m in jnp.split(mod, 6, axis=-1)]
        if t >= tm:
            gate_rows = t
            as_gate = lambda m: m
        else:
            gate_rows = rows
            as_gate = lambda m: jnp.broadcast_to(m, (b, t, D_MODEL)).reshape(1, rows, D_MODEL)
        grp = jnp.full((n_tiles,), l, jnp.int32)

        h = _norm_mod(xr, p['norm1_g'][l], as_gate(sc1), as_gate(sh1), tr=tr, rows_per_gate=gate_rows)
        proj_a = _gmm(h, [p['w_in']], grp, tm=tm, tn=512, out_dtype=F32, n_cols=OFF_DT, name="proj_a")
        proj_b = _gmm(h, [p['w_in_b']], grp, tm=tm, tn=512, out_dtype=F32, name="proj_b")
        proj_dt = _gmm(h, [p['w_in_dt']], grp, tm=tm, tn=LANES, out_dtype=F32, name="proj_dt")
        proj_a = proj_a.reshape(b, t, OFF_DT)
        proj_b = proj_b.reshape(b, t, D_IN - OFF_POOL)
        proj_dt = proj_dt.reshape(b, t, LANES)[..., :SSD_HEADS]
        seg = lambda i: proj_b[..., i * D_GROUP:(i + 1) * D_GROUP]

        y_ssd, conv_new, ssd_new = _ssd_mixer(
            proj_a[..., OFF_Z:OFF_XBC], proj_a[..., OFF_XBC:OFF_DT], proj_dt,
            conv_buf, ssd_h0, p['ssd_conv_w'][l], p['ssd_conv_b'][l], p['ssd_dt_bias'][l],
            p['ssd_a_log'][l], p['ssd_d'][l], p['ssd_norm_g'][l])
        y_pool, pool_new = _pool_mixer(seg(0), pool_buf, pos0, p['pool_w'][l], p['pool_scale'][l])
        if cache is None:
            y_s5n, re_new, im_new = _s5_prompt(proj_b, p, l, col0=D_GROUP // S5_CH_BLOCK)
            y_att, k_new = _moba_prompt(proj_b, p['attn_q_g'][l], p['attn_k_g'][l], col0=2)
            k_new = k_new.reshape(b, t, ATTN_HEADS, ATTN_HEADDIM)
            v_new = seg(4).reshape(b, t, ATTN_HEADS, ATTN_HEADDIM)
        else:
            y_s5, re_new, im_new = _s5_mixer(
                seg(1), s5_h0_re, s5_h0_im, p['s5_a_re'][l], p['s5_a_im'][l],
                p['s5_log_dt'][l], p['s5_b_re'][l], p['s5_b_im'][l], p['s5_c_re'][l], p['s5_c_im'][l],
                p['s5_d'][l], p['s5_glu_w'][l], p['s5_glu_b'][l])
            y_s5n = _rmsnorm(y_s5, p['s5_norm_g'][l]).reshape(rows, D_GROUP).astype(BF16)
            y_att, k_new, v_new = _moba_mixer(seg(2), seg(3), seg(4), k_past, v_past, pos0,
                                              p['attn_q_g'][l], p['attn_k_g'][l])
        to_rows = lambda v: v.reshape(rows, D_GROUP).astype(BF16)
        mixed_in = jnp.concatenate([to_rows(y_ssd),
                                    to_rows(_rmsnorm(y_pool, p['pool_norm_g'][l])),
                                    y_s5n,
                                    to_rows(_rmsnorm(y_att, p['attn_norm_g'][l]))], axis=-1)
        xr = _gmm(mixed_in, [p['w_out']], grp, tm=tm, tn=512, out_dtype=F32, name="w_out",
                  gate=as_gate(g1), rows_per_gate=gate_rows, res=xr)

        if l % 2 == 0:
            h2 = _norm_mod(xr, p['norm2_g'][l], as_gate(sc2), as_gate(sh2), tr=tr, rows_per_gate=gate_rows)
            xr = _dense_ffn(h2, xr, as_gate(g2), gate_rows, p['ffn_w1'], p['ffn_w3'], p['ffn_w2'], l // 2, tm=tm)
        else:
            h2, logits = _norm_mod(xr, p['norm2_g'][l], as_gate(sc2), as_gate(sh2), tr=tr,
                                   rows_per_gate=gate_rows, router_w=p['moe_router_w'][l // 2])
            g2_rows = jnp.broadcast_to(g2, (b, t, D_MODEL)).reshape(rows, D_MODEL)
            xr = _moe_ffn(h2, logits, xr, g2_rows, p['moe_router_b'][l // 2],
                          p['moe_w1'], p['moe_w3'], p['moe_w2'], l // 2, tm=moe_tm)
        for lst, val in zip(new, (k_new, v_new, ssd_new, conv_new, pool_new, re_new, im_new)):
            lst.append(val)
    return xr.reshape(b, t, D_MODEL), [jnp.stack(v, axis=0) for v in new]


def kernel(x_prompt, x_sample, cache_k, cache_v, state_ssd, state_ssd_conv, state_pool,
           state_s5_re, state_s5_im, page_table, c_prompt, c_sample,
           ada_w, ada_b, norm1_g, norm2_g, w_in, w_out,
           ssd_conv_w, ssd_conv_b, ssd_dt_bias, ssd_a_log, ssd_d, ssd_norm_g,
           pool_w, pool_scale, pool_norm_g,
           s5_a_re, s5_a_im, s5_log_dt, s5_b_re, s5_b_im, s5_c_re, s5_c_im,
           s5_d, s5_glu_w, s5_glu_b, s5_norm_g,
           attn_q_g, attn_k_g, attn_norm_g,
           ffn_w1, ffn_w3, ffn_w2,
           moe_router_w, moe_router_b, moe_w1, moe_w3, moe_w2):
    w_in_b = w_in[:, :, OFF_POOL:]
    w_in_dt = jnp.pad(w_in[:, :, OFF_DT:OFF_POOL], ((0, 0), (0, 0), (0, LANES - SSD_HEADS)))
    p = dict(norm1_g=norm1_g, norm2_g=norm2_g, w_in=w_in, w_in_b=w_in_b, w_in_dt=w_in_dt, w_out=w_out,
             ssd_conv_w=ssd_conv_w, ssd_conv_b=ssd_conv_b, ssd_dt_bias=ssd_dt_bias,
             ssd_a_log=ssd_a_log, ssd_d=ssd_d, ssd_norm_g=ssd_norm_g,
             pool_w=pool_w, pool_scale=pool_scale, pool_norm_g=pool_norm_g,
             s5_a_re=s5_a_re, s5_a_im=s5_a_im, s5_log_dt=s5_log_dt, s5_b_re=s5_b_re, s5_b_im=s5_b_im,
             s5_c_re=s5_c_re, s5_c_im=s5_c_im, s5_d=s5_d, s5_glu_w=s5_glu_w, s5_glu_b=s5_glu_b,
             s5_norm_g=s5_norm_g, attn_q_g=attn_q_g, attn_k_g=attn_k_g, attn_norm_g=attn_norm_g,
             ffn_w1=ffn_w1, ffn_w3=ffn_w3, ffn_w2=ffn_w2,
             moe_router_w=moe_router_w, moe_router_b=moe_router_b,
             moe_w1=moe_w1, moe_w3=moe_w3, moe_w2=moe_w2)

    c_all = jnp.concatenate([c_prompt, c_sample], axis=0)
    n_c = c_all.shape[0]
    c_pad = jnp.pad(jax.nn.silu(c_all), ((0, BF16_SUBLANES - n_c), (0, 0))).astype(BF16)
    mods_p, mods_s = [], []
    for l in range(DEPTH):
        m = _gmm(c_pad, [ada_w], jnp.full((1,), l, jnp.int32), tm=BF16_SUBLANES, tn=512, out_dtype=F32,
                 name="ada_mod")
        m = m[:n_c] + ada_b[l]
        mods_p.append(m[:BATCH])
        mods_s.append(m[BATCH:])

    y_prompt, st_p = _run_trunk(x_prompt, mods_p, 0, p, None, None)
    cache = (cache_k, cache_v, state_ssd, state_ssd_conv, state_pool, state_s5_re, state_s5_im)
    y_sample, st_s = _run_trunk(x_sample, mods_s, PAST_LEN, p, cache, page_table)
    k_p, v_p, ssd_p, conv_p, pool_p, re_p, im_p = st_p
    k_s, v_s, ssd_s, conv_s, pool_s, re_s, im_s = st_s
    return (y_prompt, y_sample, k_p, v_p, ssd_p, conv_p, pool_p, re_p, im_p,
            k_s, v_s, ssd_s, conv_s, pool_s, re_s, im_s)
```

```python
import functools
import math

import jax
import jax.numpy as jnp
from jax import lax
from jax.experimental import pallas as pl
from jax.experimental.pallas import tpu as pltpu

D_MODEL = 4096
BATCH = 4
SEQ = 2048
DEPTH = 4
DEC_BATCH = 8
DEC_SEQ = 4
PAST_LEN = 8192
PAGE_SIZE = 128
D_GROUP = D_MODEL // 4
SSD_HEADDIM = 64
SSD_HEADS = D_GROUP // SSD_HEADDIM
SSD_STATE = 128
SSD_GROUPS = 2
SSD_CONV = 4
SSD_CHUNK = 128
SSD_CONV_DIM = D_GROUP + 2 * SSD_GROUPS * SSD_STATE
POOL_WINDOWS = (2, 4, 8, 16)
POOL_GROUP = D_GROUP // len(POOL_WINDOWS)
POOL_BUF = max(POOL_WINDOWS) - 1
S5_CH = 16
S5_GROUPS = D_GROUP // S5_CH
S5_STATE = 64
ATTN_HEADDIM = 128
ATTN_HEADS = D_GROUP // ATTN_HEADDIM
MOBA_BLOCK = 256
MOBA_TOPK = 3
MOBA_QCHUNK = 16
D_FF = 11008
N_EXPERTS = 8
TOP_K = 2
D_FF_EXPERT = D_FF // 2
EPS = 1e-6
OFF_Z = 0
OFF_XBC = OFF_Z + D_GROUP
OFF_DT = OFF_XBC + SSD_CONV_DIM
OFF_POOL = OFF_DT + SSD_HEADS
OFF_S5 = OFF_POOL + D_GROUP
OFF_Q = OFF_S5 + D_GROUP
OFF_K = OFF_Q + D_GROUP
OFF_V = OFF_K + D_GROUP
D_IN = OFF_V + D_GROUP

F32 = jnp.float32
BF16 = jnp.bfloat16
LANES = 128
BF16_SUBLANES = 16
VMEM_LIMIT_BYTES = 56 * 1024 * 1024


def _gmm_kernel(*refs, n_w, has_res, has_gate, has_partial, has_rowscale, has_valid):
    it = iter(refs)
    te_ref, tv_ref, x_ref = next(it), next(it), next(it)
    w_refs = [next(it) for _ in range(n_w)]
    res_ref = next(it) if has_res else None
    gate_ref = next(it) if has_gate else None
    partial_ref = next(it) if has_partial else None
    rowscale_ref = next(it) if has_rowscale else None
    o_ref = next(it)
    wbf_refs = [next(it) for _ in range(n_w)]

    i = pl.program_id(1)
    prev = te_ref[jnp.maximum(i - 1, 0)]
    group_changed = jnp.logical_or(i == 0, te_ref[i] != prev)

    @pl.when(group_changed)
    def _():
        for w_ref, wbf_ref in zip(w_refs, wbf_refs):
            wbf_ref[...] = w_ref[0].astype(BF16)

    def compute():
        x = x_ref[...]
        acc = jnp.dot(x, wbf_refs[0][...], preferred_element_type=F32)
        if n_w == 2:
            up = jnp.dot(x, wbf_refs[1][...], preferred_element_type=F32)
            acc = (acc * (1.0 / (1.0 + jnp.exp(-acc)))) * up
        if has_partial:
            acc = partial_ref[...] + acc
        if has_rowscale:
            acc = rowscale_ref[...] * acc
        if has_gate:
            acc = gate_ref[0] * acc
        if has_res:
            acc = res_ref[...] + acc
        o_ref[...] = acc.astype(o_ref.dtype)

    if has_valid:
        pl.when(tv_ref[i] != 0)(compute)

        @pl.when(tv_ref[i] == 0)
        def _():
            o_ref[...] = jnp.zeros_like(o_ref)
    else:
        compute()


def _gmm(x, ws, tile_group, *, tm, tn, out_dtype, name, n_cols=None, x_col_block=0, k_dim=None,
         res=None, gate=None, rows_per_gate=None, partial=None, rowscale=None, tile_valid=None):
    p_rows = x.shape[0]
    k = ws[0].shape[1] if k_dim is None else k_dim
    n_cols = ws[0].shape[2] if n_cols is None else n_cols
    n_row_tiles = p_rows // tm
    assert n_row_tiles * tm == p_rows and tile_group.shape == (n_row_tiles,)
    has_valid = tile_valid is not None
    if tile_valid is None:
        tile_valid = jnp.ones((n_row_tiles,), jnp.int32)
    grid = (pl.cdiv(n_cols, tn), n_row_tiles)

    in_specs = [pl.BlockSpec((tm, k), lambda j, i, te, tv: (i, x_col_block))]
    args = [x]
    for w in ws:
        in_specs.append(pl.BlockSpec((1, k, tn), lambda j, i, te, tv: (te[i], 0, j)))
        args.append(w)
    if res is not None:
        in_specs.append(pl.BlockSpec((tm, tn), lambda j, i, te, tv: (i, j)))
        args.append(res)
    if gate is not None:
        rg = gate.shape[1]
        in_specs.append(pl.BlockSpec((1, rg, tn), lambda j, i, te, tv: ((i * tm) // rows_per_gate, 0, j)))
        args.append(gate)
    if partial is not None:
        in_specs.append(pl.BlockSpec((tm, tn), lambda j, i, te, tv: (i, j)))
        args.append(partial)
    if rowscale is not None:
        in_specs.append(pl.BlockSpec((tm, 1), lambda j, i, te, tv: (i, 0)))
        args.append(rowscale)

    kernel = functools.partial(
        _gmm_kernel, n_w=len(ws), has_res=res is not None, has_gate=gate is not None,
        has_partial=partial is not None, has_rowscale=rowscale is not None, has_valid=has_valid)
    return pl.pallas_call(
        kernel,
        name=name,
        grid_spec=pltpu.PrefetchScalarGridSpec(
            num_scalar_prefetch=2,
            grid=grid,
            in_specs=in_specs,
            out_specs=pl.BlockSpec((tm, tn), lambda j, i, te, tv: (i, j)),
            scratch_shapes=[pltpu.VMEM((k, tn), BF16) for _ in ws],
        ),
        out_shape=jax.ShapeDtypeStruct((p_rows, n_cols), out_dtype),
        compiler_params=pltpu.CompilerParams(
            dimension_semantics=("arbitrary", "arbitrary"),
            vmem_limit_bytes=VMEM_LIMIT_BYTES),
    )(tile_group, tile_valid, *args)


def _norm_mod_kernel(*refs, has_router):
    if has_router:
        x_ref, g_ref, sc_ref, sh_ref, rw_ref, h_ref, logit_ref = refs
    else:
        x_ref, g_ref, sc_ref, sh_ref, h_ref = refs
    xf = x_ref[...]
    ms = jnp.mean(xf * xf, axis=-1, keepdims=True)
    y = xf * lax.rsqrt(ms + EPS) * g_ref[...]
    h = y * (1.0 + sc_ref[0]) + sh_ref[0]
    h_ref[...] = h.astype(BF16)
    if has_router:
        logit_ref[...] = jnp.dot(h, rw_ref[...], preferred_element_type=F32,
                                 precision=lax.Precision.HIGHEST)


def _norm_mod(x, g, sc, sh, *, tr, rows_per_gate, router_w=None):
    r, d = x.shape
    rg = sc.shape[1]
    mod_spec = pl.BlockSpec((1, rg, d), lambda i: ((i * tr) // rows_per_gate, 0, 0))
    in_specs = [pl.BlockSpec((tr, d), lambda i: (i, 0)), pl.BlockSpec((1, d), lambda i: (0, 0)),
                mod_spec, mod_spec]
    args = [x, g.reshape(1, d), sc, sh]
    out_shape = [jax.ShapeDtypeStruct((r, d), BF16)]
    out_specs = [pl.BlockSpec((tr, d), lambda i: (i, 0))]
    if router_w is not None:
        rw = jnp.pad(router_w, ((0, 0), (0, LANES - router_w.shape[1])))
        in_specs.append(pl.BlockSpec((d, LANES), lambda i: (0, 0)))
        args.append(rw)
        out_shape.append(jax.ShapeDtypeStruct((r, LANES), F32))
        out_specs.append(pl.BlockSpec((tr, LANES), lambda i: (i, 0)))
    out = pl.pallas_call(
        functools.partial(_norm_mod_kernel, has_router=router_w is not None),
        name="norm_mod",
        grid=(r // tr,),
        in_specs=in_specs,
        out_specs=out_specs,
        out_shape=out_shape,
        compiler_params=pltpu.CompilerParams(
            dimension_semantics=("arbitrary",), vmem_limit_bytes=VMEM_LIMIT_BYTES),
    )(*args)
    return out if router_w is not None else out[0]


def _qk_norm_kernel(q_ref, k_ref, v_ref, qg_ref, kg_ref, qn_ref, kn_ref, knb_ref, vb_ref, km_ref):
    for hd in range(ATTN_HEADS):
        sl = slice(hd * ATTN_HEADDIM, (hd + 1) * ATTN_HEADDIM)
        q = q_ref[0, :, sl]
        k = k_ref[0, :, sl]
        qn = q * lax.rsqrt(jnp.mean(q * q, axis=-1, keepdims=True) + EPS) * qg_ref[...]
        kn = k * lax.rsqrt(jnp.mean(k * k, axis=-1, keepdims=True) + EPS) * kg_ref[...]
        qn_ref[0, :, sl] = qn
        kn_ref[0, :, sl] = kn
        knb_ref[0, :, sl] = kn.astype(BF16)
        km_ref[0, :, sl] = jnp.mean(kn, axis=0, keepdims=True)
    vb_ref[0] = v_ref[0].astype(BF16)


def _moba_kernel(slopes_ref, q_ref, k_ref, v_ref, km_ref, o_ref):
    hd = pl.program_id(1)
    n = pl.program_id(2)
    blk_rows = MOBA_BLOCK
    q = q_ref[0]
    km = km_ref[0]
    gate = lax.dot_general(q, km, (((1,), (1,)), ((), ())), preferred_element_type=F32,
                           precision=lax.Precision.HIGHEST)
    blk_id = lax.broadcasted_iota(jnp.int32, gate.shape, 1)
    past = blk_id < n
    gm = jnp.where(past, gate, -jnp.inf)
    rank = jnp.zeros(gate.shape, jnp.int32)
    for m in range(gate.shape[1]):
        col = gm[:, m:m + 1]
        beats = jnp.logical_or(col > gm, jnp.logical_and(col == gm, m < blk_id))
        rank = rank + beats.astype(jnp.int32)
    sel = jnp.logical_and(past, rank < MOBA_TOPK).astype(F32)

    qb = q.astype(BF16)
    slope = slopes_ref[hd]
    scale = 1.0 / math.sqrt(ATTN_HEADDIM)
    r_id = lax.broadcasted_iota(jnp.int32, (blk_rows, blk_rows), 0)
    c_id = lax.broadcasted_iota(jnp.int32, (blk_rows, blk_rows), 1)
    rc = (r_id - c_id).astype(F32)

    def scores(j):
        kj = k_ref[0, pl.ds(pl.multiple_of(j * blk_rows, blk_rows), blk_rows), :]
        return lax.dot_general(qb, kj, (((1,), (1,)), ((), ())), preferred_element_type=F32) * scale

    def values(j):
        return v_ref[0, pl.ds(pl.multiple_of(j * blk_rows, blk_rows), blk_rows), :]

    s = scores(n) - slope * rc
    s = jnp.where(c_id <= r_id, s, -jnp.inf)
    m0 = jnp.max(s, axis=-1, keepdims=True)
    p = jnp.exp(s - m0)
    l0 = jnp.sum(p, axis=-1, keepdims=True)
    acc0 = jnp.dot(p.astype(BF16), values(n), preferred_element_type=F32)

    def body(j, carry):
        m_run, l_run, acc = carry
        dist = rc + ((n - j) * blk_rows).astype(F32)
        sj = scores(j) - slope * dist
        sel_j = jnp.sum(jnp.where(blk_id == j, sel, 0.0), axis=-1, keepdims=True) > 0.5
        sj = jnp.where(sel_j, sj, -jnp.inf)
        m_new = jnp.maximum(m_run, jnp.max(sj, axis=-1, keepdims=True))
        alpha = jnp.exp(m_run - m_new)
        pj = jnp.exp(sj - m_new)
        l_new = alpha * l_run + jnp.sum(pj, axis=-1, keepdims=True)
        acc = alpha * acc + jnp.dot(pj.astype(BF16), values(j), preferred_element_type=F32)
        return m_new, l_new, acc

    _, l_fin, acc = lax.fori_loop(0, n, body, (m0, l0, acc0))
    o_ref[0] = acc / l_fin


def _moba_prompt(proj, q_g, k_g, *, col0):
    b, t, _ = proj.shape
    nblk = t // MOBA_BLOCK
    assert nblk * MOBA_BLOCK == t
    row_spec = lambda c: pl.BlockSpec((1, MOBA_BLOCK, D_GROUP), lambda bi, ni, c=c: (bi, ni, c))
    g_spec = pl.BlockSpec((1, ATTN_HEADDIM), lambda bi, ni: (0, 0))
    out_spec = pl.BlockSpec((1, MOBA_BLOCK, D_GROUP), lambda bi, ni: (bi, ni, 0))
    qn, kn, knb, vb, km = pl.pallas_call(
        _qk_norm_kernel,
        name="qk_norm",
        grid=(b, nblk),
        in_specs=[row_spec(col0), row_spec(col0 + 1), row_spec(col0 + 2), g_spec, g_spec],
        out_specs=[out_spec, out_spec, out_spec, out_spec,
                   pl.BlockSpec((1, 1, D_GROUP), lambda bi, ni: (bi * nblk + ni, 0, 0))],
        out_shape=[jax.ShapeDtypeStruct((b, t, D_GROUP), F32), jax.ShapeDtypeStruct((b, t, D_GROUP), F32),
                   jax.ShapeDtypeStruct((b, t, D_GROUP), BF16), jax.ShapeDtypeStruct((b, t, D_GROUP), BF16),
                   jax.ShapeDtypeStruct((b * nblk, 1, D_GROUP), F32)],
        compiler_params=pltpu.CompilerParams(
            dimension_semantics=("arbitrary", "arbitrary"), vmem_limit_bytes=VMEM_LIMIT_BYTES),
    )(proj, proj, proj, q_g.reshape(1, ATTN_HEADDIM), k_g.reshape(1, ATTN_HEADDIM))
    km = km.reshape(b, nblk, D_GROUP)

    slopes = _alibi_slopes(ATTN_HEADS)
    o = pl.pallas_call(
        _moba_kernel,
        name="moba_attend",
        grid_spec=pltpu.PrefetchScalarGridSpec(
            num_scalar_prefetch=1,
            grid=(b, ATTN_HEADS, nblk),
            in_specs=[
                pl.BlockSpec((1, MOBA_BLOCK, ATTN_HEADDIM), lambda bi, hi, ni, sl: (bi, ni, hi)),
                pl.BlockSpec((1, t, ATTN_HEADDIM), lambda bi, hi, ni, sl: (bi, 0, hi)),
                pl.BlockSpec((1, t, ATTN_HEADDIM), lambda bi, hi, ni, sl: (bi, 0, hi)),
                pl.BlockSpec((1, nblk, ATTN_HEADDIM), lambda bi, hi, ni, sl: (bi, 0, hi)),
            ],
            out_specs=pl.BlockSpec((1, MOBA_BLOCK, ATTN_HEADDIM), lambda bi, hi, ni, sl: (bi, ni, hi)),
        ),
        out_shape=jax.ShapeDtypeStruct((b, t, D_GROUP), F32),
        compiler_params=pltpu.CompilerParams(
            dimension_semantics=("arbitrary", "arbitrary", "arbitrary"), vmem_limit_bytes=VMEM_LIMIT_BYTES),
    )(slopes, qn, knb, vb, km)
    return o, kn


S5_LANE_BLOCK = 512
S5_CH_BLOCK = S5_LANE_BLOCK // S5_STATE * S5_CH
S5_N_BLOCKS = D_GROUP // S5_CH_BLOCK
S5_ROWS = 8
S5_MM_ROWS = 256


def _gelu_tanh(x):
    return 0.5 * x * (1.0 + jnp.tanh(math.sqrt(2.0 / math.pi) * (x + 0.044715 * (x * x * x))))


def _s5_kernel(u_ref, win_ref, tab_ref, wc_ref, d_ref, y_ref, hre_ref, him_ref, bre, bim, *, t_len):
    nl = S5_LANE_BLOCK

    def in_mm(c, carry):
        rows = pl.ds(pl.multiple_of(c * S5_MM_ROWS, S5_MM_ROWS), S5_MM_ROWS)
        r = jnp.dot(u_ref[0, rows, :].astype(BF16), win_ref[0], preferred_element_type=F32)
        bre[rows, :] = r[:, :nl]
        bim[rows, :] = r[:, nl:]
        return carry

    lax.fori_loop(0, t_len // S5_MM_ROWS, in_mm, 0)

    pre = tab_ref[0, 0:8, :]
    pim = tab_ref[0, 8:16, :]
    steps = [(1, tab_ref[0, 16:17, :], tab_ref[0, 17:18, :]),
             (2, tab_ref[0, 18:19, :], tab_ref[0, 19:20, :]),
             (4, tab_ref[0, 20:21, :], tab_ref[0, 21:22, :])]
    row = lax.broadcasted_iota(jnp.int32, (S5_ROWS, nl), 0)

    def scan_rows(i, carry):
        cr, ci = carry
        rows = pl.ds(pl.multiple_of(i * S5_ROWS, S5_ROWS), S5_ROWS)
        xr = bre[rows, :]
        xi = bim[rows, :]
        for s, ar, ai in steps:
            sr = jnp.where(row >= s, pltpu.roll(xr, s, 0), 0.0)
            si = jnp.where(row >= s, pltpu.roll(xi, s, 0), 0.0)
            xr, xi = xr + (ar * sr - ai * si), xi + (ar * si + ai * sr)
        xr, xi = xr + (pre * cr - pim * ci), xi + (pre * ci + pim * cr)
        bre[rows, :] = xr
        bim[rows, :] = xi
        return (jnp.broadcast_to(xr[S5_ROWS - 1:S5_ROWS, :], (S5_ROWS, nl)),
                jnp.broadcast_to(xi[S5_ROWS - 1:S5_ROWS, :], (S5_ROWS, nl)))

    zero = jnp.zeros((S5_ROWS, nl), F32)
    cr, ci = lax.fori_loop(0, t_len // S5_ROWS, scan_rows, (zero, zero), unroll=2)
    hre_ref[0] = cr[0:1, :]
    him_ref[0] = ci[0:1, :]

    def out_mm(c, carry):
        rows = pl.ds(pl.multiple_of(c * S5_MM_ROWS, S5_MM_ROWS), S5_MM_ROWS)
        hcat = jnp.concatenate([bre[rows, :], bim[rows, :]], axis=1).astype(BF16)
        y = jnp.dot(hcat, wc_ref[0], preferred_element_type=F32)
        y = y + d_ref[...] * u_ref[0, rows, :]
        y_ref[0, rows, :] = _gelu_tanh(y)
        return carry

    lax.fori_loop(0, t_len // S5_MM_ROWS, out_mm, 0)


def _s5_tables(a_re, a_im, log_dt, b_re, b_im, c_re, c_im):
    dt = jnp.exp(log_dt.astype(F32))[:, None]
    ar, ai = a_re.astype(F32), a_im.astype(F32)
    mag = jnp.exp(dt * ar)
    abar_re, abar_im = mag * jnp.cos(dt * ai), mag * jnp.sin(dt * ai)
    den = ar * ar + ai * ai
    nr, ni = abar_re - 1.0, abar_im
    f_re, f_im = (nr * ar + ni * ai) / den, (ni * ar - nr * ai) / den
    bb_re = f_re[..., None] * b_re - f_im[..., None] * b_im
    bb_im = f_re[..., None] * b_im + f_im[..., None] * b_re

    def cmul(x, y):
        return x[0] * y[0] - x[1] * y[1], x[0] * y[1] + x[1] * y[0]

    a1 = (abar_re.reshape(-1), abar_im.reshape(-1))
    pows = [a1]
    for _ in range(S5_ROWS - 1):
        pows.append(cmul(pows[-1], a1))
    a2 = pows[1]
    a4 = pows[3]
    rows = ([p[0] for p in pows] + [p[1] for p in pows] +
            [a1[0], a1[1], a2[0], a2[1], a4[0], a4[1]] + [jnp.zeros_like(a1[0])] * 2)
    tab = jnp.stack(rows, axis=0)
    tab = tab.reshape(24, S5_N_BLOCKS, S5_LANE_BLOCK).transpose(1, 0, 2)

    gpb = S5_LANE_BLOCK // S5_STATE
    eye = jnp.eye(gpb, dtype=F32)

    def pack_in(bb):
        x = bb.reshape(S5_N_BLOCKS, gpb, S5_STATE, S5_CH)
        x = jnp.einsum('kgnc,gh->kgchn', x, eye)
        return x.reshape(S5_N_BLOCKS, gpb * S5_CH, gpb * S5_STATE)

    def pack_out(cc):
        x = cc.reshape(S5_N_BLOCKS, gpb, S5_CH, S5_STATE)
        x = jnp.einsum('kgcn,gh->kgnhc', x, eye)
        return x.reshape(S5_N_BLOCKS, gpb * S5_STATE, gpb * S5_CH)

    w_in = jnp.concatenate([pack_in(bb_re), pack_in(bb_im)], axis=2).astype(BF16)
    w_c = jnp.concatenate([pack_out(c_re.astype(F32)), -pack_out(c_im.astype(F32))], axis=1).astype(BF16)
    return tab, w_in, w_c


def _s5_post_kernel(y_ref, w_ref, b_ref, g_ref, o_ref, wbf_ref):
    @pl.when(pl.program_id(0) == 0)
    def _():
        wbf_ref[...] = w_ref[...].astype(BF16)

    y = y_ref[...]
    z = jnp.dot(y.astype(BF16), wbf_ref[...], preferred_element_type=F32) + b_ref[...]
    y = y * (1.0 / (1.0 + jnp.exp(-z)))
    y = y * lax.rsqrt(jnp.mean(y * y, axis=-1, keepdims=True) + EPS) * g_ref[...]
    o_ref[...] = y.astype(o_ref.dtype)


def _s5_prompt(proj, p, l, *, col0):
    b, t, _ = proj.shape
    tab, w_in, w_c = _s5_tables(p['s5_a_re'][l], p['s5_a_im'][l], p['s5_log_dt'][l],
                                p['s5_b_re'][l], p['s5_b_im'][l], p['s5_c_re'][l], p['s5_c_im'][l])
    nl = S5_LANE_BLOCK
    y, hre, him = pl.pallas_call(
        functools.partial(_s5_kernel, t_len=t),
        name="s5_scan",
        grid=(b, S5_N_BLOCKS),
        in_specs=[
            pl.BlockSpec((1, t, S5_CH_BLOCK), lambda bi, ki: (bi, 0, col0 + ki)),
            pl.BlockSpec((1, S5_CH_BLOCK, 2 * nl), lambda bi, ki: (ki, 0, 0)),
            pl.BlockSpec((1, 24, nl), lambda bi, ki: (ki, 0, 0)),
            pl.BlockSpec((1, 2 * nl, S5_CH_BLOCK), lambda bi, ki: (ki, 0, 0)),
            pl.BlockSpec((1, S5_CH_BLOCK), lambda bi, ki: (0, ki)),
        ],
        out_specs=[
            pl.BlockSpec((1, t, S5_CH_BLOCK), lambda bi, ki: (bi, 0, ki)),
            pl.BlockSpec((1, 1, nl), lambda bi, ki: (bi, 0, ki)),
            pl.BlockSpec((1, 1, nl), lambda bi, ki: (bi, 0, ki)),
        ],
        out_shape=[jax.ShapeDtypeStruct((b, t, D_GROUP), F32),
                   jax.ShapeDtypeStruct((b, 1, S5_GROUPS * S5_STATE), F32),
                   jax.ShapeDtypeStruct((b, 1, S5_GROUPS * S5_STATE), F32)],
        scratch_shapes=[pltpu.VMEM((t, nl), F32), pltpu.VMEM((t, nl), F32)],
        compiler_params=pltpu.CompilerParams(
            dimension_semantics=("arbitrary", "arbitrary"), vmem_limit_bytes=VMEM_LIMIT_BYTES),
    )(proj, w_in, tab, w_c, p['s5_d'][l].reshape(1, D_GROUP))

    rows = b * t
    tr = 256
    yn = pl.pallas_call(
        _s5_post_kernel,
        name="s5_post",
        grid=(rows // tr,),
        in_specs=[pl.BlockSpec((tr, D_GROUP), lambda i: (i, 0)),
                  pl.BlockSpec((D_GROUP, D_GROUP), lambda i: (0, 0)),
                  pl.BlockSpec((1, D_GROUP), lambda i: (0, 0)),
                  pl.BlockSpec((1, D_GROUP), lambda i: (0, 0))],
        out_specs=pl.BlockSpec((tr, D_GROUP), lambda i: (i, 0)),
        out_shape=jax.ShapeDtypeStruct((rows, D_GROUP), BF16),
        scratch_shapes=[pltpu.VMEM((D_GROUP, D_GROUP), BF16)],
        compiler_params=pltpu.CompilerParams(
            dimension_semantics=("arbitrary",), vmem_limit_bytes=VMEM_LIMIT_BYTES),
    )(y.reshape(rows, D_GROUP), p['s5_glu_w'][l], p['s5_glu_b'][l].reshape(1, D_GROUP),
      p['s5_norm_g'][l].reshape(1, D_GROUP))
    return yn, hre.reshape(b, S5_GROUPS, S5_STATE), him.reshape(b, S5_GROUPS, S5_STATE)


SSD_PAIRS = SSD_HEADS // 2
SSD_HIST = 8


def _sigmoid(x):
    return 1.0 / (1.0 + jnp.exp(-x))


def _ssd_kernel(pa_ref, dt_ref, cw_ref, cb_ref, dtb_ref, alog_ref, dsk_ref, g_ref, sel_ref,
                y_ref, hlast_ref, ext_ref, state_ref, ybuf_ref, *, n_chunks):
    c = pl.program_id(1)
    ln = SSD_CHUNK
    hi = lax.Precision.HIGHEST

    @pl.when(c == 0)
    def _():
        ext_ref[0:SSD_HIST, :] = jnp.zeros((SSD_HIST, SSD_CONV_DIM), F32)
        state_ref[...] = jnp.zeros_like(state_ref)

    ext_ref[SSD_HIST:SSD_HIST + ln, :] = pa_ref[0, :, OFF_XBC:OFF_DT]
    acc = cb_ref[...]
    for k in range(SSD_CONV):
        off = SSD_HIST - (SSD_CONV - 1) + k
        acc = acc + ext_ref[off:off + ln, :] * cw_ref[k:k + 1, :]
    ext_ref[0:SSD_HIST, :] = ext_ref[ln:ln + SSD_HIST, :]
    xc = acc * _sigmoid(acc)
    gn = SSD_GROUPS * SSD_STATE
    bm = xc[:, D_GROUP:D_GROUP + gn].astype(BF16)
    cm = xc[:, D_GROUP + gn:].astype(BF16)

    dtr = dt_ref[0] + dtb_ref[...]
    dt = jnp.maximum(dtr, 0.0) + jnp.log1p(jnp.exp(-jnp.abs(dtr)))
    da = dt * (-jnp.exp(alog_ref[...]))
    r_id = lax.broadcasted_iota(jnp.int32, (ln, ln), 0)
    c_id = lax.broadcasted_iota(jnp.int32, (ln, ln), 1)
    causal = c_id <= r_id
    acum = jnp.dot(causal.astype(F32), da, preferred_element_type=F32, precision=hi)
    last = acum[ln - 1:ln, :]
    acum_t = acum.T
    dt_t = dt.T
    sel = sel_ref[...]
    w_x = jnp.dot(jnp.exp(last - acum) * dt, sel, preferred_element_type=F32, precision=hi)
    ea_x = jnp.dot(jnp.exp(acum), sel, preferred_element_type=F32, precision=hi)
    lane = lax.broadcasted_iota(jnp.int32, (ln, LANES), 1)
    first_head = lane < SSD_HEADDIM

    scores = []
    for g in range(SSD_GROUPS):
        sl = slice(g * SSD_STATE, (g + 1) * SSD_STATE)
        scores.append(lax.dot_general(cm[:, sl], bm[:, sl], (((1,), (1,)), ((), ())),
                                      preferred_element_type=F32))

    for pr in range(SSD_PAIRS):
        g = (2 * pr) // (SSD_HEADS // SSD_GROUPS)
        gsl = slice(g * SSD_STATE, (g + 1) * SSD_STATE)
        col = slice(pr * LANES, (pr + 1) * LANES)
        x_pair = xc[:, col]
        x_pair_b = x_pair.astype(BF16)
        yd = []
        for hd in (2 * pr, 2 * pr + 1):
            diff = jnp.broadcast_to(acum[:, hd:hd + 1], (ln, ln)) - acum_t[hd:hd + 1, :]
            decay = jnp.exp(jnp.where(causal, diff, -jnp.inf))
            m = (scores[g] * decay) * dt_t[hd:hd + 1, :]
            yd.append(jnp.dot(m.astype(BF16), x_pair_b, preferred_element_type=F32))
        y_diag = jnp.where(first_head, yd[0], yd[1])
        st = state_ref[pr]
        y_off = jnp.dot(cm[:, gsl], st.astype(BF16), preferred_element_type=F32) * ea_x[:, col]
        xw = (x_pair * w_x[:, col]).astype(BF16)
        st_new = lax.dot_general(bm[:, gsl], xw, (((0,), (0,)), ((), ())), preferred_element_type=F32)
        state_ref[pr] = st * ea_x[ln - 1:ln, col] + st_new
        y = (y_diag + y_off) + x_pair * dsk_ref[:, col]
        z = pa_ref[0, :, pr * LANES:(pr + 1) * LANES]
        ybuf_ref[:, col] = y * (z * _sigmoid(z))

    yv = ybuf_ref[...]
    yv = yv * lax.rsqrt(jnp.mean(yv * yv, axis=-1, keepdims=True) + EPS) * g_ref[...]
    y_ref[0] = yv.astype(y_ref.dtype)

    @pl.when(c == n_chunks - 1)
    def _():
        for pr in range(SSD_PAIRS):
            hlast_ref[0, pr] = state_ref[pr].T


def _ssd_prompt(proj_a, proj_dt, p, l):
    b, t, _ = proj_a.shape
    n_chunks = t // SSD_CHUNK
    assert n_chunks * SSD_CHUNK == t
    pad_h = lambda v: jnp.pad(v.astype(F32), (0, LANES - SSD_HEADS)).reshape(1, LANES)
    sel = (jnp.arange(LANES, dtype=jnp.int32)[:, None] ==
           (jnp.arange(D_GROUP, dtype=jnp.int32) // SSD_HEADDIM)[None, :]).astype(F32)
    full = lambda shape: pl.BlockSpec(shape, lambda bi, ci: (0,) * len(shape))
    y, hlast = pl.pallas_call(
        functools.partial(_ssd_kernel, n_chunks=n_chunks),
        name="ssd_mixer",
        grid=(b, n_chunks),
        in_specs=[
            pl.BlockSpec((1, SSD_CHUNK, OFF_DT), lambda bi, ci: (bi, ci, 0)),
            pl.BlockSpec((1, SSD_CHUNK, LANES), lambda bi, ci: (bi, ci, 0)),
            full((SSD_CONV, SSD_CONV_DIM)), full((1, SSD_CONV_DIM)),
            full((1, LANES)), full((1, LANES)), full((1, D_GROUP)), full((1, D_GROUP)),
            full((LANES, D_GROUP)),
        ],
        out_specs=[
            pl.BlockSpec((1, SSD_CHUNK, D_GROUP), lambda bi, ci: (bi, ci, 0)),
            pl.BlockSpec((1, SSD_PAIRS, LANES, SSD_STATE), lambda bi, ci: (bi, 0, 0, 0)),
        ],
        out_shape=[jax.ShapeDtypeStruct((b, t, D_GROUP), BF16),
                   jax.ShapeDtypeStruct((b, SSD_PAIRS, LANES, SSD_STATE), F32)],
        scratch_shapes=[pltpu.VMEM((SSD_HIST + SSD_CHUNK, SSD_CONV_DIM), F32),
                        pltpu.VMEM((SSD_PAIRS, SSD_STATE, LANES), F32),
                        pltpu.VMEM((SSD_CHUNK, D_GROUP), F32)],
        compiler_params=pltpu.CompilerParams(
            dimension_semantics=("arbitrary", "arbitrary"), vmem_limit_bytes=VMEM_LIMIT_BYTES),
    )(proj_a, proj_dt, p['ssd_conv_w'][l], p['ssd_conv_b'][l].reshape(1, SSD_CONV_DIM),
      pad_h(p['ssd_dt_bias'][l]), pad_h(p['ssd_a_log'][l]),
      jnp.repeat(p['ssd_d'][l].astype(F32), SSD_HEADDIM).reshape(1, D_GROUP),
      p['ssd_norm_g'][l].reshape(1, D_GROUP), sel)
    return y.reshape(b * t, D_GROUP), hlast.reshape(b, SSD_HEADS, SSD_HEADDIM, SSD_STATE)


POOL_CHUNK = 256
POOL_HIST = 16


def _pool_kernel(u_ref, w_ref, scale_ref, g_ref, o_ref, ext_ref, ybuf_ref):
    c = pl.program_id(1)
    ln = POOL_CHUNK

    @pl.when(c == 0)
    def _():
        ext_ref[0:POOL_HIST, :] = jnp.zeros((POOL_HIST, D_GROUP), F32)

    ext_ref[POOL_HIST:POOL_HIST + ln, :] = u_ref[0]
    pos = c * ln + lax.broadcasted_iota(jnp.int32, (ln, 1), 0)
    for gi, win in enumerate(POOL_WINDOWS):
        col = slice(gi * POOL_GROUP, (gi + 1) * POOL_GROUP)
        u = ext_ref[POOL_HIST:POOL_HIST + ln, col]
        tot = u
        for k in range(1, win):
            tot = tot + ext_ref[POOL_HIST - k:POOL_HIST - k + ln, col]
        cnt = jnp.minimum(win, pos + 1).astype(F32)
        d = tot / cnt - u
        y = jnp.dot(d.astype(BF16), w_ref[gi].astype(BF16), preferred_element_type=F32)
        ybuf_ref[:, col] = y * scale_ref[:, col]
    ext_ref[0:POOL_HIST, :] = ext_ref[ln:ln + POOL_HIST, :]
    yv = ybuf_ref[...]
    yv = yv * lax.rsqrt(jnp.mean(yv * yv, axis=-1, keepdims=True) + EPS) * g_ref[...]
    o_ref[0] = yv.astype(o_ref.dtype)


def _pool_prompt(proj, p, l, *, col0):
    b, t, _ = proj.shape
    full = lambda shape: pl.BlockSpec(shape, lambda bi, ci: (0,) * len(shape))
    y = pl.pallas_call(
        _pool_kernel,
        name="pool_mixer",
        grid=(b, t // POOL_CHUNK),
        in_specs=[pl.BlockSpec((1, POOL_CHUNK, D_GROUP), lambda bi, ci: (bi, ci, col0)),
                  full((len(POOL_WINDOWS), POOL_GROUP, POOL_GROUP)), full((1, D_GROUP)), full((1, D_GROUP))],
        out_specs=pl.BlockSpec((1, POOL_CHUNK, D_GROUP), lambda bi, ci: (bi, ci, 0)),
        out_shape=jax.ShapeDtypeStruct((b, t, D_GROUP), BF16),
        scratch_shapes=[pltpu.VMEM((POOL_HIST + POOL_CHUNK, D_GROUP), F32),
                        pltpu.VMEM((POOL_CHUNK, D_GROUP), F32)],
        compiler_params=pltpu.CompilerParams(
            dimension_semantics=("arbitrary", "arbitrary"), vmem_limit_bytes=VMEM_LIMIT_BYTES),
    )(proj, p['pool_w'][l], p['pool_scale'][l].reshape(1, D_GROUP), p['pool_norm_g'][l].reshape(1, D_GROUP))
    return y.reshape(b * t, D_GROUP)


def _rmsnorm(x, g):
    xf = x.astype(F32)
    xf = xf * lax.rsqrt(jnp.mean(xf * xf, axis=-1, keepdims=True) + EPS)
    return (xf * g.astype(F32)).astype(x.dtype)


def _alibi_slopes(n):
    return 2.0 ** (-8.0 * jnp.arange(1, n + 1, dtype=F32) / n)


def _causal_depthwise_conv(ext, w, bias, t):
    out = bias
    for k in range(w.shape[0]):
        out = out + ext[:, k:k + t] * w[k]
    return out


def _ssd_chunked_scan(x, dt, a, bm, cm, h0, chunk):
    b, t, h, p = x.shape
    nc = t // chunk
    rep = h // SSD_GROUPS
    bh = jnp.repeat(bm, rep, axis=2).astype(F32)
    ch = jnp.repeat(cm, rep, axis=2).astype(F32)
    da = dt * a
    resh = lambda v: v.reshape((b, nc, chunk) + v.shape[2:])
    xc, dtc, ac, bc, cc = resh(x.astype(F32)), resh(dt), resh(da), resh(bh), resh(ch)
    acum = jnp.cumsum(ac, axis=2)
    diff = acum[:, :, :, None, :] - acum[:, :, None, :, :]
    causal = jnp.tril(jnp.ones((chunk, chunk), dtype=bool))
    decay = jnp.exp(jnp.where(causal[None, None, :, :, None], diff, -jnp.inf))
    scores = jnp.einsum('bcihn,bcjhn->bcijh', cc, bc)
    y_diag = jnp.einsum('bcijh,bcjhp->bcihp', scores * decay * dtc[:, :, None], xc)
    decay_end = jnp.exp(acum[:, :, -1:, :] - acum)
    states = jnp.einsum('bcjhn,bcjh,bcjhp->bchpn', bc, decay_end * dtc, xc)
    chunk_decay = jnp.exp(acum[:, :, -1, :])

    def step(hh, inp):
        dec, st = inp
        return hh * dec[:, :, None, None] + st, hh

    h_last, h_start = lax.scan(step, h0.astype(F32),
                               (jnp.moveaxis(chunk_decay, 1, 0), jnp.moveaxis(states, 1, 0)))
    h_start = jnp.moveaxis(h_start, 0, 1)
    y_off = jnp.einsum('bcihn,bchpn->bcihp', cc, h_start) * jnp.exp(acum)[..., None]
    y = (y_diag + y_off).reshape(b, t, h, p)
    return y.astype(x.dtype), h_last.astype(h0.dtype)


def _ssd_mixer(z, xbc, dt_raw, conv_buf, h0, conv_w, conv_b, dt_bias, a_log, d_skip, norm_g):
    b, t, _ = xbc.shape
    ext = jnp.concatenate([conv_buf.astype(xbc.dtype), xbc], axis=1)
    new_buf = ext[:, t:]
    xbc_c = jax.nn.silu(_causal_depthwise_conv(ext, conv_w, conv_b, t))
    gn = SSD_GROUPS * SSD_STATE
    xs = xbc_c[..., :D_GROUP].reshape(b, t, SSD_HEADS, SSD_HEADDIM)
    bm = xbc_c[..., D_GROUP:D_GROUP + gn].reshape(b, t, SSD_GROUPS, SSD_STATE)
    cm = xbc_c[..., D_GROUP + gn:].reshape(b, t, SSD_GROUPS, SSD_STATE)
    dt = jax.nn.softplus((dt_raw + dt_bias).astype(F32))
    a = -jnp.exp(a_log.astype(F32))
    chunk = SSD_CHUNK if t % SSD_CHUNK == 0 else t
    y, h_last = _ssd_chunked_scan(xs, dt, a, bm, cm, h0, chunk)
    y = (y + xs * d_skip[:, None]).reshape(b, t, D_GROUP)
    y = _rmsnorm(y * jax.nn.silu(z), norm_g)
    return y, new_buf, h_last


def _pool_mixer(u, buf, pos0, pool_w, pool_scale):
    b, t, c = u.shape
    ext = jnp.concatenate([buf.astype(u.dtype), u], axis=1)
    new_buf = ext[:, t:]
    cs = jnp.cumsum(ext.astype(F32), axis=1)
    cs = jnp.concatenate([jnp.zeros_like(cs[:, :1]), cs], axis=1)
    pos = pos0 + jnp.arange(t, dtype=jnp.int32)
    outs = []
    for g, w in enumerate(POOL_WINDOWS):
        sl = slice(g * POOL_GROUP, (g + 1) * POOL_GROUP)
        end = cs[:, POOL_BUF + 1:POOL_BUF + 1 + t, sl]
        start = cs[:, POOL_BUF + 1 - w:POOL_BUF + 1 - w + t, sl]
        cnt = jnp.minimum(w, pos + 1).astype(F32)[None, :, None]
        d = ((end - start) / cnt - u[..., sl].astype(F32)).astype(u.dtype)
        outs.append(d @ pool_w[g])
    return jnp.concatenate(outs, axis=-1) * pool_scale, new_buf


def _s5_mixer(u, h0_re, h0_im, a_re, a_im, log_dt, b_re, b_im, c_re, c_im, d_skip, glu_w, glu_b):
    b, t, c = u.shape
    ug = u.reshape(b, t, S5_GROUPS, S5_CH).astype(F32)
    dt = jnp.exp(log_dt.astype(F32))[:, None]
    ar, ai = a_re.astype(F32), a_im.astype(F32)
    mag = jnp.exp(dt * ar)
    abar_re, abar_im = mag * jnp.cos(dt * ai), mag * jnp.sin(dt * ai)
    den = ar * ar + ai * ai
    nr, ni = abar_re - 1.0, abar_im
    f_re, f_im = (nr * ar + ni * ai) / den, (ni * ar - nr * ai) / den
    bb_re = f_re[..., None] * b_re - f_im[..., None] * b_im
    bb_im = f_re[..., None] * b_im + f_im[..., None] * b_re
    bu_re = jnp.einsum('gnc,btgc->btgn', bb_re, ug)
    bu_im = jnp.einsum('gnc,btgc->btgn', bb_im, ug)
    a_full_re = jnp.broadcast_to(abar_re, bu_re.shape)
    a_full_im = jnp.broadcast_to(abar_im, bu_im.shape)

    def combine(e1, e2):
        a1r, a1i, b1r, b1i = e1
        a2r, a2i, b2r, b2i = e2
        return (a2r * a1r - a2i * a1i, a2r * a1i + a2i * a1r,
                a2r * b1r - a2i * b1i + b2r, a2r * b1i + a2i * b1r + b2i)

    _, _, hr, hi = lax.associative_scan(combine, (a_full_re, a_full_im, bu_re, bu_im), axis=1)
    k = jnp.arange(1, t + 1, dtype=F32)[:, None, None]
    pm = jnp.exp(k * dt * ar)
    pw_re, pw_im = pm * jnp.cos(k * dt * ai), pm * jnp.sin(k * dt * ai)
    h0r, h0i = h0_re.astype(F32)[:, None], h0_im.astype(F32)[:, None]
    hr = hr + pw_re * h0r - pw_im * h0i
    hi = hi + pw_re * h0i + pw_im * h0r
    y = jnp.einsum('gcn,btgn->btgc', c_re, hr) - jnp.einsum('gcn,btgn->btgc', c_im, hi)
    y = y.reshape(b, t, c) + d_skip * u.astype(F32)
    y = jax.nn.gelu(y)
    y = y * jax.nn.sigmoid(y @ glu_w.astype(F32) + glu_b)
    return y.astype(u.dtype), hr[:, -1].astype(h0_re.dtype), hi[:, -1].astype(h0_im.dtype)


def _moba_attend(q, k_all, v_all, q_pos):
    b, tq, h, dh = q.shape
    l = k_all.shape[1]
    nb = -(-l // MOBA_BLOCK)
    pad = nb * MOBA_BLOCK - l
    padw = ((0, 0), (0, pad), (0, 0), (0, 0))
    kb = jnp.pad(k_all, padw).reshape(b, nb, MOBA_BLOCK, h, dh).transpose(0, 3, 1, 2, 4)
    vb = jnp.pad(v_all, padw).reshape(b, nb, MOBA_BLOCK, h, dh).transpose(0, 3, 1, 2, 4)
    kmean = jnp.mean(kb.astype(F32), axis=3)
    slopes = _alibi_slopes(h)
    scale = 1.0 / math.sqrt(dh)
    bi = jnp.arange(b)[:, None, None, None]
    hi = jnp.arange(h)[None, :, None, None]
    blk_ids = jnp.arange(nb, dtype=jnp.int32)
    offs = jnp.arange(MOBA_BLOCK, dtype=jnp.int32)

    def attend_chunk(args):
        qc, pc = args
        n_q = pc.shape[0]
        own = pc // MOBA_BLOCK
        gate = jnp.einsum('bhqd,bhnd->bhqn', qc.astype(F32), kmean)
        gate = jnp.where(blk_ids[None, :] < own[:, None], gate, -jnp.inf)
        if nb < MOBA_TOPK:
            gate = jnp.pad(gate, ((0, 0), (0, 0), (0, 0), (0, MOBA_TOPK - nb)), constant_values=-jnp.inf)
        g_val, g_idx = lax.top_k(gate, MOBA_TOPK)
        sel = jnp.concatenate([jnp.minimum(g_idx, nb - 1),
                               jnp.broadcast_to(own[None, None, :, None], (b, h, n_q, 1))], axis=-1)
        sel_ok = jnp.concatenate([jnp.isfinite(g_val), jnp.ones((b, h, n_q, 1), dtype=bool)], axis=-1)
        kg = kb[bi, hi, sel]
        vg = vb[bi, hi, sel]
        s = jnp.einsum('bhqd,bhqsnd->bhqsn', qc, kg).astype(F32) * scale
        kpos = sel[..., None] * MOBA_BLOCK + offs
        qpos = pc[None, None, :, None, None]
        s = s - slopes[None, :, None, None, None] * (qpos - kpos).astype(F32)
        s = jnp.where(sel_ok[..., None] & (kpos <= qpos), s, -jnp.inf)
        p = jax.nn.softmax(s.reshape(b, h, n_q, -1), axis=-1).reshape(s.shape)
        return jnp.einsum('bhqsn,bhqsnd->bhqd', p.astype(vg.dtype), vg)

    qlen = MOBA_QCHUNK if tq % MOBA_QCHUNK == 0 else tq
    nq = tq // qlen
    qs = jnp.moveaxis(q.transpose(0, 2, 1, 3).reshape(b, h, nq, qlen, dh), 2, 0)
    ps = q_pos.reshape(nq, qlen)
    o = lax.map(attend_chunk, (qs, ps))
    return jnp.moveaxis(o, 0, 2).reshape(b, h, tq, dh).transpose(0, 2, 1, 3)


def _moba_mixer(q, k, v, k_past, v_past, pos0, q_g, k_g):
    b, t, _ = q.shape
    shp = (b, t, ATTN_HEADS, ATTN_HEADDIM)
    q = _rmsnorm(q.reshape(shp), q_g)
    k = _rmsnorm(k.reshape(shp), k_g)
    v = v.reshape(shp)
    if k_past is None:
        k_all, v_all = k, v
    else:
        k_all = jnp.concatenate([k_past.astype(k.dtype), k], axis=1)
        v_all = jnp.concatenate([v_past.astype(v.dtype), v], axis=1)
    o = _moba_attend(q, k_all, v_all, pos0 + jnp.arange(t, dtype=jnp.int32))
    return o.reshape(b, t, D_GROUP), k, v


def _dense_ffn(h2, x_res, g2, rows_per_gate, w1, w3, w2, l2, *, tm):
    rows = h2.shape[0]
    n_tiles = rows // tm
    grp = jnp.full((n_tiles,), l2, jnp.int32)
    act = _gmm(h2, [w1, w3], grp, tm=tm, tn=512, out_dtype=BF16, name="ffn_up")
    w2h = w2.reshape(w2.shape[0] * 2, D_FF_EXPERT, D_MODEL)
    part = _gmm(act, [w2h], grp * 2, tm=tm, tn=512, out_dtype=F32, x_col_block=0, k_dim=D_FF_EXPERT,
                name="ffn_down_lo")
    return _gmm(act, [w2h], grp * 2 + 1, tm=tm, tn=512, out_dtype=F32, x_col_block=1, k_dim=D_FF_EXPERT,
                name="ffn_down_hi", partial=part, gate=g2, rows_per_gate=rows_per_gate, res=x_res)


def _moe_ffn(h2, logits, router_b, w1, w3, w2, l2, *, tm):
    n_tok = h2.shape[0]
    lg = logits[:, :N_EXPERTS] + router_b.astype(F32)
    top_v, top_i = lax.top_k(lg, TOP_K)
    gates = jax.nn.softmax(top_v, axis=-1)

    n_pairs = n_tok * TOP_K
    n_tiles = -(-n_pairs // tm) + N_EXPERTS
    p_rows = n_tiles * tm
    e_flat = top_i.reshape(-1).astype(jnp.int32)
    onehot = (e_flat[:, None] == jnp.arange(N_EXPERTS, dtype=jnp.int32)[None, :]).astype(jnp.int32)
    rank = jnp.sum((jnp.cumsum(onehot, axis=0) - onehot) * onehot, axis=1)
    counts = jnp.sum(onehot, axis=0)
    tiles_e = (counts + tm - 1) // tm
    tile_end = jnp.cumsum(tiles_e)
    starts = (tile_end - tiles_e) * tm
    pos = starts[e_flat] + rank
    tok = jnp.arange(n_pairs, dtype=jnp.int32) // TOP_K
    row_token = jnp.zeros((p_rows,), jnp.int32).at[pos].set(tok)
    row_gate = jnp.zeros((p_rows,), F32).at[pos].set(gates.reshape(-1))
    tile_ids = jnp.arange(n_tiles, dtype=jnp.int32)
    total_tiles = tile_end[-1]
    tile_valid = (tile_ids < total_tiles).astype(jnp.int32)
    clipped = jnp.minimum(tile_ids, total_tiles - 1)
    tile_expert = jnp.sum((clipped[:, None] >= tile_end[None, :]).astype(jnp.int32), axis=1)
    tile_group = (l2 * N_EXPERTS + tile_expert).astype(jnp.int32)

    xs = jnp.take(h2, row_token, axis=0)
    n_moe = w1.shape[0]
    w1r = w1.reshape(n_moe * N_EXPERTS, D_MODEL, D_FF_EXPERT)
    w3r = w3.reshape(n_moe * N_EXPERTS, D_MODEL, D_FF_EXPERT)
    w2r = w2.reshape(n_moe * N_EXPERTS, D_FF_EXPERT, D_MODEL)
    act = _gmm(xs, [w1r, w3r], tile_group, tm=tm, tn=512, out_dtype=BF16, name="moe_up", tile_valid=tile_valid)
    ys = _gmm(act, [w2r], tile_group, tm=tm, tn=512, out_dtype=F32, name="moe_down",
              rowscale=row_gate.reshape(p_rows, 1), tile_valid=tile_valid)
    pos2 = pos.reshape(n_tok, TOP_K)
    return jnp.take(ys, pos2[:, 0], axis=0) + jnp.take(ys, pos2[:, 1], axis=0)


def _layer(l, xr, b, t, mod, pos0, p, cache, page_table):
    rows = b * t
    tm = 512 if rows % 512 == 0 else rows
    tr = 256 if rows % 256 == 0 else rows
    n_tiles = rows // tm
    if True:
        if cache is None:
            k_past = v_past = None
        else:
            cache_k, cache_v, st_ssd, st_conv, st_pool, st_re, st_im = cache
            k_past = cache_k[l][page_table].reshape(b, -1, ATTN_HEADS, ATTN_HEADDIM)
            v_past = cache_v[l][page_table].reshape(b, -1, ATTN_HEADS, ATTN_HEADDIM)
            ssd_h0, conv_buf, pool_buf = st_ssd[l], st_conv[l], st_pool[l]
            s5_h0_re, s5_h0_im = st_re[l], st_im[l]
        sh1, sc1, g1, sh2, sc2, g2 = [m.reshape(b, 1, D_MODEL) for m in jnp.split(mod, 6, axis=-1)]
        if t >= tm:
            gate_rows = t
            as_gate = lambda m: m
        else:
            gate_rows = rows
            as_gate = lambda m: jnp.broadcast_to(m, (b, t, D_MODEL)).reshape(1, rows, D_MODEL)
        grp = jnp.full((n_tiles,), l, jnp.int32)

        h = _norm_mod(xr, p['norm1_g'][l], as_gate(sc1), as_gate(sh1), tr=tr, rows_per_gate=gate_rows)
        proj_a = _gmm(h, [p['w_in']], grp, tm=tm, tn=512, out_dtype=F32, n_cols=OFF_DT, name="proj_a")
        proj_b = _gmm(h, [p['w_in_b']], grp, tm=tm, tn=512, out_dtype=F32, name="proj_b")
        proj_dt = _gmm(h, [p['w_in_dt']], grp, tm=tm, tn=LANES, out_dtype=F32, name="proj_dt")
        proj_a = proj_a.reshape(b, t, OFF_DT)
        proj_b = proj_b.reshape(b, t, D_IN - OFF_POOL)
        proj_dt = proj_dt.reshape(b, t, LANES)
        seg = lambda i: proj_b[..., i * D_GROUP:(i + 1) * D_GROUP]
        to_rows = lambda v: v.reshape(rows, D_GROUP).astype(BF16)

        if cache is None:
            y_ssdn, ssd_new = _ssd_prompt(proj_a, proj_dt, p, l)
            conv_new = proj_a[:, t - (SSD_CONV - 1):, OFF_XBC:OFF_DT]
            y_pooln = _pool_prompt(proj_b, p, l, col0=0)
            pool_new = proj_b[:, t - POOL_BUF:, :D_GROUP]
            y_s5n, re_new, im_new = _s5_prompt(proj_b, p, l, col0=D_GROUP // S5_CH_BLOCK)
            y_att, k_new = _moba_prompt(proj_b, p['attn_q_g'][l], p['attn_k_g'][l], col0=2)
            k_new = k_new.reshape(b, t, ATTN_HEADS, ATTN_HEADDIM)
            v_new = seg(4).reshape(b, t, ATTN_HEADS, ATTN_HEADDIM)
        else:
            y_ssd, conv_new, ssd_new = _ssd_mixer(
                proj_a[..., OFF_Z:OFF_XBC], proj_a[..., OFF_XBC:OFF_DT], proj_dt[..., :SSD_HEADS],
                conv_buf, ssd_h0, p['ssd_conv_w'][l], p['ssd_conv_b'][l], p['ssd_dt_bias'][l],
                p['ssd_a_log'][l], p['ssd_d'][l], p['ssd_norm_g'][l])
            y_ssdn = to_rows(y_ssd)
            y_pool, pool_new = _pool_mixer(seg(0), pool_buf, pos0, p['pool_w'][l], p['pool_scale'][l])
            y_pooln = to_rows(_rmsnorm(y_pool, p['pool_norm_g'][l]))
            y_s5, re_new, im_new = _s5_mixer(
                seg(1), s5_h0_re, s5_h0_im, p['s5_a_re'][l], p['s5_a_im'][l],
                p['s5_log_dt'][l], p['s5_b_re'][l], p['s5_b_im'][l], p['s5_c_re'][l], p['s5_c_im'][l],
                p['s5_d'][l], p['s5_glu_w'][l], p['s5_glu_b'][l])
            y_s5n = _rmsnorm(y_s5, p['s5_norm_g'][l]).reshape(rows, D_GROUP).astype(BF16)
            y_att, k_new, v_new = _moba_mixer(seg(2), seg(3), seg(4), k_past, v_past, pos0,
                                              p['attn_q_g'][l], p['attn_k_g'][l])
        mixed_in = jnp.concatenate([y_ssdn,
                                    y_pooln,
                                    y_s5n,
                                    to_rows(_rmsnorm(y_att, p['attn_norm_g'][l]))], axis=-1)
        xr = _gmm(mixed_in, [p['w_out']], grp, tm=tm, tn=512, out_dtype=F32, name="w_out",
                  gate=as_gate(g1), rows_per_gate=gate_rows, res=xr)

        states = (k_new, v_new, ssd_new, conv_new, pool_new, re_new, im_new)
        if l % 2 == 0:
            h2 = _norm_mod(xr, p['norm2_g'][l], as_gate(sc2), as_gate(sh2), tr=tr, rows_per_gate=gate_rows)
            xr = _dense_ffn(h2, xr, as_gate(g2), gate_rows, p['ffn_w1'], p['ffn_w3'], p['ffn_w2'], l // 2, tm=tm)
            return xr, states, None
        h2, logits = _norm_mod(xr, p['norm2_g'][l], as_gate(sc2), as_gate(sh2), tr=tr,
                               rows_per_gate=gate_rows, router_w=p['moe_router_w'][l // 2])
        g2_rows = jnp.broadcast_to(g2, (b, t, D_MODEL)).reshape(rows, D_MODEL)
        return xr, states, (h2, logits, g2_rows)


def kernel(x_prompt, x_sample, cache_k, cache_v, state_ssd, state_ssd_conv, state_pool,
           state_s5_re, state_s5_im, page_table, c_prompt, c_sample,
           ada_w, ada_b, norm1_g, norm2_g, w_in, w_out,
           ssd_conv_w, ssd_conv_b, ssd_dt_bias, ssd_a_log, ssd_d, ssd_norm_g,
           pool_w, pool_scale, pool_norm_g,
           s5_a_re, s5_a_im, s5_log_dt, s5_b_re, s5_b_im, s5_c_re, s5_c_im,
           s5_d, s5_glu_w, s5_glu_b, s5_norm_g,
           attn_q_g, attn_k_g, attn_norm_g,
           ffn_w1, ffn_w3, ffn_w2,
           moe_router_w, moe_router_b, moe_w1, moe_w3, moe_w2):
    w_in_b = w_in[:, :, OFF_POOL:]
    w_in_dt = jnp.pad(w_in[:, :, OFF_DT:OFF_POOL], ((0, 0), (0, 0), (0, LANES - SSD_HEADS)))
    p = dict(norm1_g=norm1_g, norm2_g=norm2_g, w_in=w_in, w_in_b=w_in_b, w_in_dt=w_in_dt, w_out=w_out,
             ssd_conv_w=ssd_conv_w, ssd_conv_b=ssd_conv_b, ssd_dt_bias=ssd_dt_bias,
             ssd_a_log=ssd_a_log, ssd_d=ssd_d, ssd_norm_g=ssd_norm_g,
             pool_w=pool_w, pool_scale=pool_scale, pool_norm_g=pool_norm_g,
             s5_a_re=s5_a_re, s5_a_im=s5_a_im, s5_log_dt=s5_log_dt, s5_b_re=s5_b_re, s5_b_im=s5_b_im,
             s5_c_re=s5_c_re, s5_c_im=s5_c_im, s5_d=s5_d, s5_glu_w=s5_glu_w, s5_glu_b=s5_glu_b,
             s5_norm_g=s5_norm_g, attn_q_g=attn_q_g, attn_k_g=attn_k_g, attn_norm_g=attn_norm_g,
             ffn_w1=ffn_w1, ffn_w3=ffn_w3, ffn_w2=ffn_w2,
             moe_router_w=moe_router_w, moe_router_b=moe_router_b,
             moe_w1=moe_w1, moe_w3=moe_w3, moe_w2=moe_w2)

    c_all = jnp.concatenate([c_prompt, c_sample], axis=0)
    n_c = c_all.shape[0]
    c_pad = jnp.pad(jax.nn.silu(c_all), ((0, BF16_SUBLANES - n_c), (0, 0))).astype(BF16)
    mods_p, mods_s = [], []
    for l in range(DEPTH):
        m = _gmm(c_pad, [ada_w], jnp.full((1,), l, jnp.int32), tm=BF16_SUBLANES, tn=512, out_dtype=F32,
                 name="ada_mod")
        m = m[:n_c] + ada_b[l]
        mods_p.append(m[:BATCH])
        mods_s.append(m[BATCH:])

    cache = (cache_k, cache_v, state_ssd, state_ssd_conv, state_pool, state_s5_re, state_s5_im)
    n_p, n_s = BATCH * SEQ, DEC_BATCH * DEC_SEQ
    xp = x_prompt.reshape(n_p, D_MODEL)
    xs = x_sample.reshape(n_s, D_MODEL)
    new_p, new_s = [[] for _ in range(7)], [[] for _ in range(7)]
    for l in range(DEPTH):
        xp, st_p, moe_p = _layer(l, xp, BATCH, SEQ, mods_p[l], 0, p, None, None)
        xs, st_s, moe_s = _layer(l, xs, DEC_BATCH, DEC_SEQ, mods_s[l], PAST_LEN, p, cache, page_table)
        if moe_p is not None:
            h2 = jnp.concatenate([moe_p[0], moe_s[0]], axis=0)
            logits = jnp.concatenate([moe_p[1], moe_s[1]], axis=0)
            f = _moe_ffn(h2, logits, moe_router_b[l // 2], moe_w1, moe_w3, moe_w2, l // 2, tm=512)
            xp = xp + moe_p[2] * f[:n_p]
            xs = xs + moe_s[2] * f[n_p:]
        for lst, val in zip(new_p, st_p):
            lst.append(val)
        for lst, val in zip(new_s, st_s):
            lst.append(val)
    y_prompt = xp.reshape(BATCH, SEQ, D_MODEL)
    y_sample = xs.reshape(DEC_BATCH, DEC_SEQ, D_MODEL)
    k_p, v_p, ssd_p, conv_p, pool_p, re_p, im_p = [jnp.stack(v, axis=0) for v in new_p]
    k_s, v_s, ssd_s, conv_s, pool_s, re_s, im_s = [jnp.stack(v, axis=0) for v in new_s]
    return (y_prompt, y_sample, k_p, v_p, ssd_p, conv_p, pool_p, re_p, im_p,
            k_s, v_s, ssd_s, conv_s, pool_s, re_s, im_s)
```

```python
import functools
import math

import jax
import jax.numpy as jnp
from jax import lax
from jax.experimental import pallas as pl
from jax.experimental.pallas import tpu as pltpu

D_MODEL = 4096
BATCH = 4
SEQ = 2048
DEPTH = 4
DEC_BATCH = 8
DEC_SEQ = 4
PAST_LEN = 8192
PAGE_SIZE = 128
D_GROUP = D_MODEL // 4
SSD_HEADDIM = 64
SSD_HEADS = D_GROUP // SSD_HEADDIM
SSD_STATE = 128
SSD_GROUPS = 2
SSD_CONV = 4
SSD_CHUNK = 128
SSD_CONV_DIM = D_GROUP + 2 * SSD_GROUPS * SSD_STATE
POOL_WINDOWS = (2, 4, 8, 16)
POOL_GROUP = D_GROUP // len(POOL_WINDOWS)
POOL_BUF = max(POOL_WINDOWS) - 1
S5_CH = 16
S5_GROUPS = D_GROUP // S5_CH
S5_STATE = 64
ATTN_HEADDIM = 128
ATTN_HEADS = D_GROUP // ATTN_HEADDIM
MOBA_BLOCK = 256
MOBA_TOPK = 3
MOBA_QCHUNK = 16
D_FF = 11008
N_EXPERTS = 8
TOP_K = 2
D_FF_EXPERT = D_FF // 2
EPS = 1e-6
OFF_Z = 0
OFF_XBC = OFF_Z + D_GROUP
OFF_DT = OFF_XBC + SSD_CONV_DIM
OFF_POOL = OFF_DT + SSD_HEADS
OFF_S5 = OFF_POOL + D_GROUP
OFF_Q = OFF_S5 + D_GROUP
OFF_K = OFF_Q + D_GROUP
OFF_V = OFF_K + D_GROUP
D_IN = OFF_V + D_GROUP

F32 = jnp.float32
BF16 = jnp.bfloat16
LANES = 128
BF16_SUBLANES = 16
VMEM_LIMIT_BYTES = 56 * 1024 * 1024


def _gmm_kernel(*refs, n_w, has_res, has_gate, has_partial, has_rowscale, has_valid):
    it = iter(refs)
    te_ref, tv_ref, x_ref = next(it), next(it), next(it)
    w_refs = [next(it) for _ in range(n_w)]
    res_ref = next(it) if has_res else None
    gate_ref = next(it) if has_gate else None
    partial_ref = next(it) if has_partial else None
    rowscale_ref = next(it) if has_rowscale else None
    o_ref = next(it)
    wbf_refs = [next(it) for _ in range(n_w)]

    i = pl.program_id(1)
    prev = te_ref[jnp.maximum(i - 1, 0)]
    group_changed = jnp.logical_or(i == 0, te_ref[i] != prev)

    @pl.when(group_changed)
    def _():
        for w_ref, wbf_ref in zip(w_refs, wbf_refs):
            wbf_ref[...] = w_ref[0].astype(BF16)

    def compute():
        x = x_ref[...]
        acc = jnp.dot(x, wbf_refs[0][...], preferred_element_type=F32)
        if n_w == 2:
            up = jnp.dot(x, wbf_refs[1][...], preferred_element_type=F32)
            acc = (acc * (1.0 / (1.0 + jnp.exp(-acc)))) * up
        if has_partial:
            acc = partial_ref[...] + acc
        if has_rowscale:
            acc = rowscale_ref[...] * acc
        if has_gate:
            acc = gate_ref[0] * acc
        if has_res:
            acc = res_ref[...] + acc
        o_ref[...] = acc.astype(o_ref.dtype)

    if has_valid:
        pl.when(tv_ref[i] != 0)(compute)

        @pl.when(tv_ref[i] == 0)
        def _():
            o_ref[...] = jnp.zeros_like(o_ref)
    else:
        compute()


def _gmm(x, ws, tile_group, *, tm, tn, out_dtype, name, n_cols=None, x_col_block=0, k_dim=None,
         res=None, gate=None, rows_per_gate=None, partial=None, rowscale=None, tile_valid=None):
    p_rows = x.shape[0]
    k = ws[0].shape[1] if k_dim is None else k_dim
    n_cols = ws[0].shape[2] if n_cols is None else n_cols
    n_row_tiles = p_rows // tm
    assert n_row_tiles * tm == p_rows and tile_group.shape == (n_row_tiles,)
    has_valid = tile_valid is not None
    if tile_valid is None:
        tile_valid = jnp.ones((n_row_tiles,), jnp.int32)
    grid = (pl.cdiv(n_cols, tn), n_row_tiles)

    in_specs = [pl.BlockSpec((tm, k), lambda j, i, te, tv: (i, x_col_block))]
    args = [x]
    for w in ws:
        in_specs.append(pl.BlockSpec((1, k, tn), lambda j, i, te, tv: (te[i], 0, j)))
        args.append(w)
    if res is not None:
        in_specs.append(pl.BlockSpec((tm, tn), lambda j, i, te, tv: (i, j)))
        args.append(res)
    if gate is not None:
        rg = gate.shape[1]
        in_specs.append(pl.BlockSpec((1, rg, tn), lambda j, i, te, tv: ((i * tm) // rows_per_gate, 0, j)))
        args.append(gate)
    if partial is not None:
        in_specs.append(pl.BlockSpec((tm, tn), lambda j, i, te, tv: (i, j)))
        args.append(partial)
    if rowscale is not None:
        in_specs.append(pl.BlockSpec((tm, 1), lambda j, i, te, tv: (i, 0)))
        args.append(rowscale)

    kernel = functools.partial(
        _gmm_kernel, n_w=len(ws), has_res=res is not None, has_gate=gate is not None,
        has_partial=partial is not None, has_rowscale=rowscale is not None, has_valid=has_valid)
    return pl.pallas_call(
        kernel,
        name=name,
        grid_spec=pltpu.PrefetchScalarGridSpec(
            num_scalar_prefetch=2,
            grid=grid,
            in_specs=in_specs,
            out_specs=pl.BlockSpec((tm, tn), lambda j, i, te, tv: (i, j)),
            scratch_shapes=[pltpu.VMEM((k, tn), BF16) for _ in ws],
        ),
        out_shape=jax.ShapeDtypeStruct((p_rows, n_cols), out_dtype),
        compiler_params=pltpu.CompilerParams(
            dimension_semantics=("arbitrary", "arbitrary"),
            vmem_limit_bytes=VMEM_LIMIT_BYTES),
    )(tile_group, tile_valid, *args)


def _norm_mod_kernel(*refs, has_router):
    if has_router:
        x_ref, g_ref, sc_ref, sh_ref, rw_ref, h_ref, logit_ref = refs
    else:
        x_ref, g_ref, sc_ref, sh_ref, h_ref = refs
    xf = x_ref[...]
    ms = jnp.mean(xf * xf, axis=-1, keepdims=True)
    y = xf * lax.rsqrt(ms + EPS) * g_ref[...]
    h = y * (1.0 + sc_ref[0]) + sh_ref[0]
    h_ref[...] = h.astype(h_ref.dtype)
    if has_router:
        logit_ref[...] = jnp.dot(h, rw_ref[...], preferred_element_type=F32,
                                 precision=lax.Precision.HIGHEST)


def _norm_mod(x, g, sc, sh, *, tr, rows_per_gate, router_w=None):
    r, d = x.shape
    rg = sc.shape[1]
    mod_spec = pl.BlockSpec((1, rg, d), lambda i: ((i * tr) // rows_per_gate, 0, 0))
    in_specs = [pl.BlockSpec((tr, d), lambda i: (i, 0)), pl.BlockSpec((1, d), lambda i: (0, 0)),
                mod_spec, mod_spec]
    args = [x, g.reshape(1, d), sc, sh]
    out_shape = [jax.ShapeDtypeStruct((r, d), BF16 if router_w is None else F32)]
    out_specs = [pl.BlockSpec((tr, d), lambda i: (i, 0))]
    if router_w is not None:
        rw = jnp.pad(router_w, ((0, 0), (0, LANES - router_w.shape[1])))
        in_specs.append(pl.BlockSpec((d, LANES), lambda i: (0, 0)))
        args.append(rw)
        out_shape.append(jax.ShapeDtypeStruct((r, LANES), F32))
        out_specs.append(pl.BlockSpec((tr, LANES), lambda i: (i, 0)))
    out = pl.pallas_call(
        functools.partial(_norm_mod_kernel, has_router=router_w is not None),
        name="norm_mod",
        grid=(r // tr,),
        in_specs=in_specs,
        out_specs=out_specs,
        out_shape=out_shape,
        compiler_params=pltpu.CompilerParams(
            dimension_semantics=("arbitrary",), vmem_limit_bytes=VMEM_LIMIT_BYTES),
    )(*args)
    return out if router_w is not None else out[0]


def _qk_norm_kernel(q_ref, k_ref, v_ref, qg_ref, kg_ref, qn_ref, kn_ref, knb_ref, vb_ref, km_ref):
    for hd in range(ATTN_HEADS):
        sl = slice(hd * ATTN_HEADDIM, (hd + 1) * ATTN_HEADDIM)
        q = q_ref[0, :, sl]
        k = k_ref[0, :, sl]
        qn = q * lax.rsqrt(jnp.mean(q * q, axis=-1, keepdims=True) + EPS) * qg_ref[...]
        kn = k * lax.rsqrt(jnp.mean(k * k, axis=-1, keepdims=True) + EPS) * kg_ref[...]
        qn_ref[0, :, sl] = qn
        kn_ref[0, :, sl] = kn
        knb_ref[0, :, sl] = kn.astype(BF16)
        km_ref[0, :, sl] = jnp.mean(kn, axis=0, keepdims=True)
    vb_ref[0] = v_ref[0].astype(BF16)


def _moba_kernel(slopes_ref, q_ref, k_ref, v_ref, km_ref, o_ref):
    hd = pl.program_id(1)
    n = pl.program_id(2)
    blk_rows = MOBA_BLOCK
    q = q_ref[0]
    km = km_ref[0]
    gate_t = lax.dot_general(km, q, (((1,), (1,)), ((), ())), preferred_element_type=F32,
                             precision=lax.Precision.HIGHEST)
    blk_t = lax.broadcasted_iota(jnp.int32, gate_t.shape, 0)
    past_t = blk_t < n
    gm = jnp.where(past_t, gate_t, -jnp.inf)
    rank = jnp.zeros(gate_t.shape, jnp.int32)
    for m in range(gate_t.shape[0]):
        row_m = gm[m:m + 1, :]
        beats = jnp.logical_or(row_m > gm, jnp.logical_and(row_m == gm, m < blk_t))
        rank = rank + beats.astype(jnp.int32)
    sel_t = jnp.logical_and(past_t, rank < MOBA_TOPK).astype(BF16)

    qb = q.astype(BF16)
    slope = slopes_ref[hd]
    scale = 1.0 / math.sqrt(ATTN_HEADDIM)
    r_id = lax.broadcasted_iota(jnp.int32, (blk_rows, blk_rows), 0)
    c_id = lax.broadcasted_iota(jnp.int32, (blk_rows, blk_rows), 1)
    rc = (r_id - c_id).astype(F32)
    eye = (r_id == c_id).astype(BF16)
    sel = lax.dot_general(eye, sel_t, (((1,), (1,)), ((), ())), preferred_element_type=F32)
    blk_id = lax.broadcasted_iota(jnp.int32, sel.shape, 1)

    def scores(j):
        kj = k_ref[0, pl.ds(pl.multiple_of(j * blk_rows, blk_rows), blk_rows), :]
        return lax.dot_general(qb, kj, (((1,), (1,)), ((), ())), preferred_element_type=F32) * scale

    def values(j):
        return v_ref[0, pl.ds(pl.multiple_of(j * blk_rows, blk_rows), blk_rows), :]

    s = scores(n) - slope * rc
    s = jnp.where(c_id <= r_id, s, -jnp.inf)
    m0 = jnp.max(s, axis=-1, keepdims=True)
    p = jnp.exp(s - m0)
    l0 = jnp.sum(p, axis=-1, keepdims=True)
    acc0 = jnp.dot(p.astype(BF16), values(n), preferred_element_type=F32)

    def body(j, carry):
        m_run, l_run, acc = carry
        dist = rc + ((n - j) * blk_rows).astype(F32)
        sj = scores(j) - slope * dist
        sel_j = jnp.sum(jnp.where(blk_id == j, sel, 0.0), axis=-1, keepdims=True) > 0.5
        sj = jnp.where(sel_j, sj, -jnp.inf)
        m_new = jnp.maximum(m_run, jnp.max(sj, axis=-1, keepdims=True))
        alpha = jnp.exp(m_run - m_new)
        pj = jnp.exp(sj - m_new)
        l_new = alpha * l_run + jnp.sum(pj, axis=-1, keepdims=True)
        acc = alpha * acc + jnp.dot(pj.astype(BF16), values(j), preferred_element_type=F32)
        return m_new, l_new, acc

    _, l_fin, acc = lax.fori_loop(0, n, body, (m0, l0, acc0))
    o_ref[0] = acc / l_fin


def _moba_prompt(proj, q_g, k_g, *, col0):
    b, t, _ = proj.shape
    nblk = t // MOBA_BLOCK
    assert nblk * MOBA_BLOCK == t
    row_spec = lambda c: pl.BlockSpec((1, MOBA_BLOCK, D_GROUP), lambda bi, ni, c=c: (bi, ni, c))
    g_spec = pl.BlockSpec((1, ATTN_HEADDIM), lambda bi, ni: (0, 0))
    out_spec = pl.BlockSpec((1, MOBA_BLOCK, D_GROUP), lambda bi, ni: (bi, ni, 0))
    qn, kn, knb, vb, km = pl.pallas_call(
        _qk_norm_kernel,
        name="qk_norm",
        grid=(b, nblk),
        in_specs=[row_spec(col0), row_spec(col0 + 1), row_spec(col0 + 2), g_spec, g_spec],
        out_specs=[out_spec, out_spec, out_spec, out_spec,
                   pl.BlockSpec((1, 1, D_GROUP), lambda bi, ni: (bi * nblk + ni, 0, 0))],
        out_shape=[jax.ShapeDtypeStruct((b, t, D_GROUP), F32), jax.ShapeDtypeStruct((b, t, D_GROUP), F32),
                   jax.ShapeDtypeStruct((b, t, D_GROUP), BF16), jax.ShapeDtypeStruct((b, t, D_GROUP), BF16),
                   jax.ShapeDtypeStruct((b * nblk, 1, D_GROUP), F32)],
        compiler_params=pltpu.CompilerParams(
            dimension_semantics=("arbitrary", "arbitrary"), vmem_limit_bytes=VMEM_LIMIT_BYTES),
    )(proj, proj, proj, q_g.reshape(1, ATTN_HEADDIM), k_g.reshape(1, ATTN_HEADDIM))
    km = km.reshape(b, nblk, D_GROUP)

    slopes = _alibi_slopes(ATTN_HEADS)
    o = pl.pallas_call(
        _moba_kernel,
        name="moba_attend",
        grid_spec=pltpu.PrefetchScalarGridSpec(
            num_scalar_prefetch=1,
            grid=(b, ATTN_HEADS, nblk),
            in_specs=[
                pl.BlockSpec((1, MOBA_BLOCK, ATTN_HEADDIM), lambda bi, hi, ni, sl: (bi, ni, hi)),
                pl.BlockSpec((1, t, ATTN_HEADDIM), lambda bi, hi, ni, sl: (bi, 0, hi)),
                pl.BlockSpec((1, t, ATTN_HEADDIM), lambda bi, hi, ni, sl: (bi, 0, hi)),
                pl.BlockSpec((1, nblk, ATTN_HEADDIM), lambda bi, hi, ni, sl: (bi, 0, hi)),
            ],
            out_specs=pl.BlockSpec((1, MOBA_BLOCK, ATTN_HEADDIM), lambda bi, hi, ni, sl: (bi, ni, hi)),
        ),
        out_shape=jax.ShapeDtypeStruct((b, t, D_GROUP), F32),
        compiler_params=pltpu.CompilerParams(
            dimension_semantics=("arbitrary", "arbitrary", "arbitrary"), vmem_limit_bytes=VMEM_LIMIT_BYTES),
    )(slopes, qn, knb, vb, km)
    return o, kn


S5_LANE_BLOCK = 512
S5_CH_BLOCK = S5_LANE_BLOCK // S5_STATE * S5_CH
S5_N_BLOCKS = D_GROUP // S5_CH_BLOCK
S5_ROWS = 8
S5_MM_ROWS = 256


def _gelu_tanh(x):
    return 0.5 * x * (1.0 + jnp.tanh(math.sqrt(2.0 / math.pi) * (x + 0.044715 * (x * x * x))))


def _s5_kernel(u_ref, win_ref, tab_ref, wc_ref, d_ref, y_ref, hre_ref, him_ref, bre, bim, *, t_len):
    nl = S5_LANE_BLOCK

    def in_mm(c, carry):
        rows = pl.ds(pl.multiple_of(c * S5_MM_ROWS, S5_MM_ROWS), S5_MM_ROWS)
        r = jnp.dot(u_ref[0, rows, :].astype(BF16), win_ref[0], preferred_element_type=F32)
        bre[rows, :] = r[:, :nl]
        bim[rows, :] = r[:, nl:]
        return carry

    lax.fori_loop(0, t_len // S5_MM_ROWS, in_mm, 0)

    pre = tab_ref[0, 0:8, :]
    pim = tab_ref[0, 8:16, :]
    row = lax.broadcasted_iota(jnp.int32, (S5_ROWS, nl), 0)
    steps = [(s, jnp.where(row >= s, tab_ref[0, i:i + 1, :], 0.0), jnp.where(row >= s, tab_ref[0, i + 1:i + 2, :], 0.0))
             for s, i in ((1, 16), (2, 18), (4, 20))]

    def scan_rows(i, carry):
        cr, ci = carry
        rows = pl.ds(pl.multiple_of(i * S5_ROWS, S5_ROWS), S5_ROWS)
        xr = bre[rows, :]
        xi = bim[rows, :]
        for s, ar, ai in steps:
            sr = pltpu.roll(xr, s, 0)
            si = pltpu.roll(xi, s, 0)
            xr, xi = xr + (ar * sr - ai * si), xi + (ar * si + ai * sr)
        xr, xi = xr + (pre * cr - pim * ci), xi + (pre * ci + pim * cr)
        bre[rows, :] = xr
        bim[rows, :] = xi
        return (jnp.broadcast_to(xr[S5_ROWS - 1:S5_ROWS, :], (S5_ROWS, nl)),
                jnp.broadcast_to(xi[S5_ROWS - 1:S5_ROWS, :], (S5_ROWS, nl)))

    zero = jnp.zeros((S5_ROWS, nl), F32)
    cr, ci = lax.fori_loop(0, t_len // S5_ROWS, scan_rows, (zero, zero), unroll=2)
    hre_ref[0] = cr[0:1, :]
    him_ref[0] = ci[0:1, :]

    def out_mm(c, carry):
        rows = pl.ds(pl.multiple_of(c * S5_MM_ROWS, S5_MM_ROWS), S5_MM_ROWS)
        hcat = jnp.concatenate([bre[rows, :], bim[rows, :]], axis=1).astype(BF16)
        y = jnp.dot(hcat, wc_ref[0], preferred_element_type=F32)
        y = y + d_ref[...] * u_ref[0, rows, :]
        y_ref[0, rows, :] = _gelu_tanh(y)
        return carry

    lax.fori_loop(0, t_len // S5_MM_ROWS, out_mm, 0)


def _s5_tables(a_re, a_im, log_dt, b_re, b_im, c_re, c_im):
    dt = jnp.exp(log_dt.astype(F32))[:, None]
    ar, ai = a_re.astype(F32), a_im.astype(F32)
    mag = jnp.exp(dt * ar)
    abar_re, abar_im = mag * jnp.cos(dt * ai), mag * jnp.sin(dt * ai)
    den = ar * ar + ai * ai
    nr, ni = abar_re - 1.0, abar_im
    f_re, f_im = (nr * ar + ni * ai) / den, (ni * ar - nr * ai) / den
    bb_re = f_re[..., None] * b_re - f_im[..., None] * b_im
    bb_im = f_re[..., None] * b_im + f_im[..., None] * b_re

    def cmul(x, y):
        return x[0] * y[0] - x[1] * y[1], x[0] * y[1] + x[1] * y[0]

    a1 = (abar_re.reshape(-1), abar_im.reshape(-1))
    pows = [a1]
    for _ in range(S5_ROWS - 1):
        pows.append(cmul(pows[-1], a1))
    a2 = pows[1]
    a4 = pows[3]
    rows = ([p[0] for p in pows] + [p[1] for p in pows] +
            [a1[0], a1[1], a2[0], a2[1], a4[0], a4[1]] + [jnp.zeros_like(a1[0])] * 2)
    tab = jnp.stack(rows, axis=0)
    tab = tab.reshape(24, S5_N_BLOCKS, S5_LANE_BLOCK).transpose(1, 0, 2)

    gpb = S5_LANE_BLOCK // S5_STATE
    eye = jnp.eye(gpb, dtype=F32)

    def pack_in(bb):
        x = bb.reshape(S5_N_BLOCKS, gpb, S5_STATE, S5_CH)
        x = jnp.einsum('kgnc,gh->kgchn', x, eye)
        return x.reshape(S5_N_BLOCKS, gpb * S5_CH, gpb * S5_STATE)

    def pack_out(cc):
        x = cc.reshape(S5_N_BLOCKS, gpb, S5_CH, S5_STATE)
        x = jnp.einsum('kgcn,gh->kgnhc', x, eye)
        return x.reshape(S5_N_BLOCKS, gpb * S5_STATE, gpb * S5_CH)

    w_in = jnp.concatenate([pack_in(bb_re), pack_in(bb_im)], axis=2).astype(BF16)
    w_c = jnp.concatenate([pack_out(c_re.astype(F32)), -pack_out(c_im.astype(F32))], axis=1).astype(BF16)
    return tab, w_in, w_c


def _s5_post_kernel(y_ref, w_ref, b_ref, g_ref, o_ref, wbf_ref):
    @pl.when(pl.program_id(0) == 0)
    def _():
        wbf_ref[...] = w_ref[...].astype(BF16)

    y = y_ref[...]
    z = jnp.dot(y.astype(BF16), wbf_ref[...], preferred_element_type=F32) + b_ref[...]
    y = y * (1.0 / (1.0 + jnp.exp(-z)))
    y = y * lax.rsqrt(jnp.mean(y * y, axis=-1, keepdims=True) + EPS) * g_ref[...]
    o_ref[...] = y.astype(o_ref.dtype)


def _s5_prompt(proj, p, l, *, col0):
    b, t, _ = proj.shape
    tab, w_in, w_c = _s5_tables(p['s5_a_re'][l], p['s5_a_im'][l], p['s5_log_dt'][l],
                                p['s5_b_re'][l], p['s5_b_im'][l], p['s5_c_re'][l], p['s5_c_im'][l])
    nl = S5_LANE_BLOCK
    y, hre, him = pl.pallas_call(
        functools.partial(_s5_kernel, t_len=t),
        name="s5_scan",
        grid=(b, S5_N_BLOCKS),
        in_specs=[
            pl.BlockSpec((1, t, S5_CH_BLOCK), lambda bi, ki: (bi, 0, col0 + ki)),
            pl.BlockSpec((1, S5_CH_BLOCK, 2 * nl), lambda bi, ki: (ki, 0, 0)),
            pl.BlockSpec((1, 24, nl), lambda bi, ki: (ki, 0, 0)),
            pl.BlockSpec((1, 2 * nl, S5_CH_BLOCK), lambda bi, ki: (ki, 0, 0)),
            pl.BlockSpec((1, S5_CH_BLOCK), lambda bi, ki: (0, ki)),
        ],
        out_specs=[
            pl.BlockSpec((1, t, S5_CH_BLOCK), lambda bi, ki: (bi, 0, ki)),
            pl.BlockSpec((1, 1, nl), lambda bi, ki: (bi, 0, ki)),
            pl.BlockSpec((1, 1, nl), lambda bi, ki: (bi, 0, ki)),
        ],
        out_shape=[jax.ShapeDtypeStruct((b, t, D_GROUP), F32),
                   jax.ShapeDtypeStruct((b, 1, S5_GROUPS * S5_STATE), F32),
                   jax.ShapeDtypeStruct((b, 1, S5_GROUPS * S5_STATE), F32)],
        scratch_shapes=[pltpu.VMEM((t, nl), F32), pltpu.VMEM((t, nl), F32)],
        compiler_params=pltpu.CompilerParams(
            dimension_semantics=("arbitrary", "arbitrary"), vmem_limit_bytes=VMEM_LIMIT_BYTES),
    )(proj, w_in, tab, w_c, p['s5_d'][l].reshape(1, D_GROUP))

    rows = b * t
    tr = 256
    yn = pl.pallas_call(
        _s5_post_kernel,
        name="s5_post",
        grid=(rows // tr,),
        in_specs=[pl.BlockSpec((tr, D_GROUP), lambda i: (i, 0)),
                  pl.BlockSpec((D_GROUP, D_GROUP), lambda i: (0, 0)),
                  pl.BlockSpec((1, D_GROUP), lambda i: (0, 0)),
                  pl.BlockSpec((1, D_GROUP), lambda i: (0, 0))],
        out_specs=pl.BlockSpec((tr, D_GROUP), lambda i: (i, 0)),
        out_shape=jax.ShapeDtypeStruct((rows, D_GROUP), BF16),
        scratch_shapes=[pltpu.VMEM((D_GROUP, D_GROUP), BF16)],
        compiler_params=pltpu.CompilerParams(
            dimension_semantics=("arbitrary",), vmem_limit_bytes=VMEM_LIMIT_BYTES),
    )(y.reshape(rows, D_GROUP), p['s5_glu_w'][l], p['s5_glu_b'][l].reshape(1, D_GROUP),
      p['s5_norm_g'][l].reshape(1, D_GROUP))
    return yn, hre.reshape(b, S5_GROUPS, S5_STATE), him.reshape(b, S5_GROUPS, S5_STATE)


SSD_PAIRS = SSD_HEADS // 2
SSD_HIST = 8


def _sigmoid(x):
    return 1.0 / (1.0 + jnp.exp(-x))


def _ssd_kernel(pa_ref, dt_ref, cw_ref, cb_ref, dtb_ref, alog_ref, dsk_ref, g_ref, sel_ref,
                y_ref, hlast_ref, ext_ref, state_ref, ybuf_ref, *, n_chunks):
    c = pl.program_id(1)
    ln = SSD_CHUNK
    hi = lax.Precision.HIGHEST

    @pl.when(c == 0)
    def _():
        ext_ref[0:SSD_HIST, :] = jnp.zeros((SSD_HIST, SSD_CONV_DIM), F32)
        state_ref[...] = jnp.zeros_like(state_ref)

    ext_ref[SSD_HIST:SSD_HIST + ln, :] = pa_ref[0, :, OFF_XBC:OFF_DT]
    acc = cb_ref[...]
    for k in range(SSD_CONV):
        off = SSD_HIST - (SSD_CONV - 1) + k
        acc = acc + ext_ref[off:off + ln, :] * cw_ref[k:k + 1, :]
    ext_ref[0:SSD_HIST, :] = ext_ref[ln:ln + SSD_HIST, :]
    xc = acc * _sigmoid(acc)
    gn = SSD_GROUPS * SSD_STATE
    bm = xc[:, D_GROUP:D_GROUP + gn].astype(BF16)
    cm = xc[:, D_GROUP + gn:].astype(BF16)

    dtr = dt_ref[0] + dtb_ref[...]
    dt = jnp.maximum(dtr, 0.0) + jnp.log1p(jnp.exp(-jnp.abs(dtr)))
    da = dt * (-jnp.exp(alog_ref[...]))
    r_id = lax.broadcasted_iota(jnp.int32, (ln, ln), 0)
    c_id = lax.broadcasted_iota(jnp.int32, (ln, ln), 1)
    causal = c_id <= r_id
    acum = jnp.dot(causal.astype(F32), da, preferred_element_type=F32, precision=hi)
    last = acum[ln - 1:ln, :]
    acum_t = acum.T
    dt_t = dt.T
    sel = sel_ref[...]
    w_x = jnp.dot(jnp.exp(last - acum) * dt, sel, preferred_element_type=F32, precision=hi)
    ea_x = jnp.dot(jnp.exp(acum), sel, preferred_element_type=F32, precision=hi)
    lane = lax.broadcasted_iota(jnp.int32, (ln, LANES), 1)
    first_head = lane < SSD_HEADDIM

    scores = []
    for g in range(SSD_GROUPS):
        sl = slice(g * SSD_STATE, (g + 1) * SSD_STATE)
        scores.append(lax.dot_general(cm[:, sl], bm[:, sl], (((1,), (1,)), ((), ())),
                                      preferred_element_type=F32))

    for pr in range(SSD_PAIRS):
        g = (2 * pr) // (SSD_HEADS // SSD_GROUPS)
        gsl = slice(g * SSD_STATE, (g + 1) * SSD_STATE)
        col = slice(pr * LANES, (pr + 1) * LANES)
        x_pair = xc[:, col]
        x_pair_b = x_pair.astype(BF16)
        yd = []
        for hd in (2 * pr, 2 * pr + 1):
            diff = jnp.broadcast_to(acum[:, hd:hd + 1], (ln, ln)) - acum_t[hd:hd + 1, :]
            decay = jnp.exp(jnp.where(causal, diff, -jnp.inf))
            m = (scores[g] * decay) * dt_t[hd:hd + 1, :]
            yd.append(jnp.dot(m.astype(BF16), x_pair_b, preferred_element_type=F32))
        y_diag = jnp.where(first_head, yd[0], yd[1])
        st = state_ref[pr]
        y_off = jnp.dot(cm[:, gsl], st.astype(BF16), preferred_element_type=F32) * ea_x[:, col]
        xw = (x_pair * w_x[:, col]).astype(BF16)
        st_new = lax.dot_general(bm[:, gsl], xw, (((0,), (0,)), ((), ())), preferred_element_type=F32)
        state_ref[pr] = st * ea_x[ln - 1:ln, col] + st_new
        y = (y_diag + y_off) + x_pair * dsk_ref[:, col]
        z = pa_ref[0, :, pr * LANES:(pr + 1) * LANES]
        ybuf_ref[:, col] = y * (z * _sigmoid(z))

    yv = ybuf_ref[...]
    yv = yv * lax.rsqrt(jnp.mean(yv * yv, axis=-1, keepdims=True) + EPS) * g_ref[...]
    y_ref[0] = yv.astype(y_ref.dtype)

    @pl.when(c == n_chunks - 1)
    def _():
        for pr in range(SSD_PAIRS):
            hlast_ref[0, pr] = state_ref[pr].T


def _ssd_prompt(proj_a, proj_dt, p, l):
    b, t, _ = proj_a.shape
    n_chunks = t // SSD_CHUNK
    assert n_chunks * SSD_CHUNK == t
    pad_h = lambda v: jnp.pad(v.astype(F32), (0, LANES - SSD_HEADS)).reshape(1, LANES)
    sel = (jnp.arange(LANES, dtype=jnp.int32)[:, None] ==
           (jnp.arange(D_GROUP, dtype=jnp.int32) // SSD_HEADDIM)[None, :]).astype(F32)
    full = lambda shape: pl.BlockSpec(shape, lambda bi, ci: (0,) * len(shape))
    y, hlast = pl.pallas_call(
        functools.partial(_ssd_kernel, n_chunks=n_chunks),
        name="ssd_mixer",
        grid=(b, n_chunks),
        in_specs=[
            pl.BlockSpec((1, SSD_CHUNK, OFF_DT), lambda bi, ci: (bi, ci, 0)),
            pl.BlockSpec((1, SSD_CHUNK, LANES), lambda bi, ci: (bi, ci, 0)),
            full((SSD_CONV, SSD_CONV_DIM)), full((1, SSD_CONV_DIM)),
            full((1, LANES)), full((1, LANES)), full((1, D_GROUP)), full((1, D_GROUP)),
            full((LANES, D_GROUP)),
        ],
        out_specs=[
            pl.BlockSpec((1, SSD_CHUNK, D_GROUP), lambda bi, ci: (bi, ci, 0)),
            pl.BlockSpec((1, SSD_PAIRS, LANES, SSD_STATE), lambda bi, ci: (bi, 0, 0, 0)),
        ],
        out_shape=[jax.ShapeDtypeStruct((b, t, D_GROUP), BF16),
                   jax.ShapeDtypeStruct((b, SSD_PAIRS, LANES, SSD_STATE), F32)],
        scratch_shapes=[pltpu.VMEM((SSD_HIST + SSD_CHUNK, SSD_CONV_DIM), F32),
                        pltpu.VMEM((SSD_PAIRS, SSD_STATE, LANES), F32),
                        pltpu.VMEM((SSD_CHUNK, D_GROUP), F32)],
        compiler_params=pltpu.CompilerParams(
            dimension_semantics=("arbitrary", "arbitrary"), vmem_limit_bytes=VMEM_LIMIT_BYTES),
    )(proj_a, proj_dt, p['ssd_conv_w'][l], p['ssd_conv_b'][l].reshape(1, SSD_CONV_DIM),
      pad_h(p['ssd_dt_bias'][l]), pad_h(p['ssd_a_log'][l]),
      jnp.repeat(p['ssd_d'][l].astype(F32), SSD_HEADDIM).reshape(1, D_GROUP),
      p['ssd_norm_g'][l].reshape(1, D_GROUP), sel)
    return y.reshape(b * t, D_GROUP), hlast.reshape(b, SSD_HEADS, SSD_HEADDIM, SSD_STATE)


POOL_CHUNK = 256
POOL_HIST = 16


def _pool_kernel(u_ref, w_ref, scale_ref, g_ref, o_ref, ext_ref, ybuf_ref):
    c = pl.program_id(1)
    ln = POOL_CHUNK

    @pl.when(c == 0)
    def _():
        ext_ref[0:POOL_HIST, :] = jnp.zeros((POOL_HIST, D_GROUP), F32)

    ext_ref[POOL_HIST:POOL_HIST + ln, :] = u_ref[0]
    pos = c * ln + lax.broadcasted_iota(jnp.int32, (ln, 1), 0)
    for gi, win in enumerate(POOL_WINDOWS):
        col = slice(gi * POOL_GROUP, (gi + 1) * POOL_GROUP)
        u = ext_ref[POOL_HIST:POOL_HIST + ln, col]
        tot = u
        for k in range(1, win):
            tot = tot + ext_ref[POOL_HIST - k:POOL_HIST - k + ln, col]
        cnt = jnp.minimum(win, pos + 1).astype(F32)
        d = tot / cnt - u
        y = jnp.dot(d.astype(BF16), w_ref[gi].astype(BF16), preferred_element_type=F32)
        ybuf_ref[:, col] = y * scale_ref[:, col]
    ext_ref[0:POOL_HIST, :] = ext_ref[ln:ln + POOL_HIST, :]
    yv = ybuf_ref[...]
    yv = yv * lax.rsqrt(jnp.mean(yv * yv, axis=-1, keepdims=True) + EPS) * g_ref[...]
    o_ref[0] = yv.astype(o_ref.dtype)


def _pool_prompt(proj, p, l, *, col0):
    b, t, _ = proj.shape
    full = lambda shape: pl.BlockSpec(shape, lambda bi, ci: (0,) * len(shape))
    y = pl.pallas_call(
        _pool_kernel,
        name="pool_mixer",
        grid=(b, t // POOL_CHUNK),
        in_specs=[pl.BlockSpec((1, POOL_CHUNK, D_GROUP), lambda bi, ci: (bi, ci, col0)),
                  full((len(POOL_WINDOWS), POOL_GROUP, POOL_GROUP)), full((1, D_GROUP)), full((1, D_GROUP))],
        out_specs=pl.BlockSpec((1, POOL_CHUNK, D_GROUP), lambda bi, ci: (bi, ci, 0)),
        out_shape=jax.ShapeDtypeStruct((b, t, D_GROUP), BF16),
        scratch_shapes=[pltpu.VMEM((POOL_HIST + POOL_CHUNK, D_GROUP), F32),
                        pltpu.VMEM((POOL_CHUNK, D_GROUP), F32)],
        compiler_params=pltpu.CompilerParams(
            dimension_semantics=("arbitrary", "arbitrary"), vmem_limit_bytes=VMEM_LIMIT_BYTES),
    )(proj, p['pool_w'][l], p['pool_scale'][l].reshape(1, D_GROUP), p['pool_norm_g'][l].reshape(1, D_GROUP))
    return y.reshape(b * t, D_GROUP)


def _rmsnorm(x, g):
    xf = x.astype(F32)
    xf = xf * lax.rsqrt(jnp.mean(xf * xf, axis=-1, keepdims=True) + EPS)
    return (xf * g.astype(F32)).astype(x.dtype)


def _alibi_slopes(n):
    return 2.0 ** (-8.0 * jnp.arange(1, n + 1, dtype=F32) / n)


def _causal_depthwise_conv(ext, w, bias, t):
    out = bias
    for k in range(w.shape[0]):
        out = out + ext[:, k:k + t] * w[k]
    return out


def _ssd_chunked_scan(x, dt, a, bm, cm, h0, chunk):
    b, t, h, p = x.shape
    nc = t // chunk
    rep = h // SSD_GROUPS
    bh = jnp.repeat(bm, rep, axis=2).astype(F32)
    ch = jnp.repeat(cm, rep, axis=2).astype(F32)
    da = dt * a
    resh = lambda v: v.reshape((b, nc, chunk) + v.shape[2:])
    xc, dtc, ac, bc, cc = resh(x.astype(F32)), resh(dt), resh(da), resh(bh), resh(ch)
    acum = jnp.cumsum(ac, axis=2)
    diff = acum[:, :, :, None, :] - acum[:, :, None, :, :]
    causal = jnp.tril(jnp.ones((chunk, chunk), dtype=bool))
    decay = jnp.exp(jnp.where(causal[None, None, :, :, None], diff, -jnp.inf))
    scores = jnp.einsum('bcihn,bcjhn->bcijh', cc, bc)
    y_diag = jnp.einsum('bcijh,bcjhp->bcihp', scores * decay * dtc[:, :, None], xc)
    decay_end = jnp.exp(acum[:, :, -1:, :] - acum)
    states = jnp.einsum('bcjhn,bcjh,bcjhp->bchpn', bc, decay_end * dtc, xc)
    chunk_decay = jnp.exp(acum[:, :, -1, :])

    def step(hh, inp):
        dec, st = inp
        return hh * dec[:, :, None, None] + st, hh

    h_last, h_start = lax.scan(step, h0.astype(F32),
                               (jnp.moveaxis(chunk_decay, 1, 0), jnp.moveaxis(states, 1, 0)))
    h_start = jnp.moveaxis(h_start, 0, 1)
    y_off = jnp.einsum('bcihn,bchpn->bcihp', cc, h_start) * jnp.exp(acum)[..., None]
    y = (y_diag + y_off).reshape(b, t, h, p)
    return y.astype(x.dtype), h_last.astype(h0.dtype)


def _ssd_mixer(z, xbc, dt_raw, conv_buf, h0, conv_w, conv_b, dt_bias, a_log, d_skip, norm_g):
    b, t, _ = xbc.shape
    ext = jnp.concatenate([conv_buf.astype(xbc.dtype), xbc], axis=1)
    new_buf = ext[:, t:]
    xbc_c = jax.nn.silu(_causal_depthwise_conv(ext, conv_w, conv_b, t))
    gn = SSD_GROUPS * SSD_STATE
    xs = xbc_c[..., :D_GROUP].reshape(b, t, SSD_HEADS, SSD_HEADDIM)
    bm = xbc_c[..., D_GROUP:D_GROUP + gn].reshape(b, t, SSD_GROUPS, SSD_STATE)
    cm = xbc_c[..., D_GROUP + gn:].reshape(b, t, SSD_GROUPS, SSD_STATE)
    dt = jax.nn.softplus((dt_raw + dt_bias).astype(F32))
    a = -jnp.exp(a_log.astype(F32))
    chunk = SSD_CHUNK if t % SSD_CHUNK == 0 else t
    y, h_last = _ssd_chunked_scan(xs, dt, a, bm, cm, h0, chunk)
    y = (y + xs * d_skip[:, None]).reshape(b, t, D_GROUP)
    y = _rmsnorm(y * jax.nn.silu(z), norm_g)
    return y, new_buf, h_last


def _pool_mixer(u, buf, pos0, pool_w, pool_scale):
    b, t, c = u.shape
    ext = jnp.concatenate([buf.astype(u.dtype), u], axis=1)
    new_buf = ext[:, t:]
    cs = jnp.cumsum(ext.astype(F32), axis=1)
    cs = jnp.concatenate([jnp.zeros_like(cs[:, :1]), cs], axis=1)
    pos = pos0 + jnp.arange(t, dtype=jnp.int32)
    outs = []
    for g, w in enumerate(POOL_WINDOWS):
        sl = slice(g * POOL_GROUP, (g + 1) * POOL_GROUP)
        end = cs[:, POOL_BUF + 1:POOL_BUF + 1 + t, sl]
        start = cs[:, POOL_BUF + 1 - w:POOL_BUF + 1 - w + t, sl]
        cnt = jnp.minimum(w, pos + 1).astype(F32)[None, :, None]
        d = ((end - start) / cnt - u[..., sl].astype(F32)).astype(u.dtype)
        outs.append(d @ pool_w[g])
    return jnp.concatenate(outs, axis=-1) * pool_scale, new_buf


def _s5_mixer(u, h0_re, h0_im, a_re, a_im, log_dt, b_re, b_im, c_re, c_im, d_skip, glu_w, glu_b):
    b, t, c = u.shape
    ug = u.reshape(b, t, S5_GROUPS, S5_CH).astype(F32)
    dt = jnp.exp(log_dt.astype(F32))[:, None]
    ar, ai = a_re.astype(F32), a_im.astype(F32)
    mag = jnp.exp(dt * ar)
    abar_re, abar_im = mag * jnp.cos(dt * ai), mag * jnp.sin(dt * ai)
    den = ar * ar + ai * ai
    nr, ni = abar_re - 1.0, abar_im
    f_re, f_im = (nr * ar + ni * ai) / den, (ni * ar - nr * ai) / den
    bb_re = f_re[..., None] * b_re - f_im[..., None] * b_im
    bb_im = f_re[..., None] * b_im + f_im[..., None] * b_re
    bu_re = jnp.einsum('gnc,btgc->btgn', bb_re, ug)
    bu_im = jnp.einsum('gnc,btgc->btgn', bb_im, ug)
    a_full_re = jnp.broadcast_to(abar_re, bu_re.shape)
    a_full_im = jnp.broadcast_to(abar_im, bu_im.shape)

    def combine(e1, e2):
        a1r, a1i, b1r, b1i = e1
        a2r, a2i, b2r, b2i = e2
        return (a2r * a1r - a2i * a1i, a2r * a1i + a2i * a1r,
                a2r * b1r - a2i * b1i + b2r, a2r * b1i + a2i * b1r + b2i)

    _, _, hr, hi = lax.associative_scan(combine, (a_full_re, a_full_im, bu_re, bu_im), axis=1)
    k = jnp.arange(1, t + 1, dtype=F32)[:, None, None]
    pm = jnp.exp(k * dt * ar)
    pw_re, pw_im = pm * jnp.cos(k * dt * ai), pm * jnp.sin(k * dt * ai)
    h0r, h0i = h0_re.astype(F32)[:, None], h0_im.astype(F32)[:, None]
    hr = hr + pw_re * h0r - pw_im * h0i
    hi = hi + pw_re * h0i + pw_im * h0r
    y = jnp.einsum('gcn,btgn->btgc', c_re, hr) - jnp.einsum('gcn,btgn->btgc', c_im, hi)
    y = y.reshape(b, t, c) + d_skip * u.astype(F32)
    y = jax.nn.gelu(y)
    y = y * jax.nn.sigmoid(y @ glu_w.astype(F32) + glu_b)
    return y.astype(u.dtype), hr[:, -1].astype(h0_re.dtype), hi[:, -1].astype(h0_im.dtype)


def _moba_attend(q, k_all, v_all, q_pos):
    b, tq, h, dh = q.shape
    l = k_all.shape[1]
    nb = -(-l // MOBA_BLOCK)
    pad = nb * MOBA_BLOCK - l
    padw = ((0, 0), (0, pad), (0, 0), (0, 0))
    kb = jnp.pad(k_all, padw).reshape(b, nb, MOBA_BLOCK, h, dh).transpose(0, 3, 1, 2, 4)
    vb = jnp.pad(v_all, padw).reshape(b, nb, MOBA_BLOCK, h, dh).transpose(0, 3, 1, 2, 4)
    kmean = jnp.mean(kb.astype(F32), axis=3)
    slopes = _alibi_slopes(h)
    scale = 1.0 / math.sqrt(dh)
    bi = jnp.arange(b)[:, None, None, None]
    hi = jnp.arange(h)[None, :, None, None]
    blk_ids = jnp.arange(nb, dtype=jnp.int32)
    offs = jnp.arange(MOBA_BLOCK, dtype=jnp.int32)

    def attend_chunk(args):
        qc, pc = args
        n_q = pc.shape[0]
        own = pc // MOBA_BLOCK
        gate = jnp.einsum('bhqd,bhnd->bhqn', qc.astype(F32), kmean)
        gate = jnp.where(blk_ids[None, :] < own[:, None], gate, -jnp.inf)
        if nb < MOBA_TOPK:
            gate = jnp.pad(gate, ((0, 0), (0, 0), (0, 0), (0, MOBA_TOPK - nb)), constant_values=-jnp.inf)
        g_val, g_idx = lax.top_k(gate, MOBA_TOPK)
        sel = jnp.concatenate([jnp.minimum(g_idx, nb - 1),
                               jnp.broadcast_to(own[None, None, :, None], (b, h, n_q, 1))], axis=-1)
        sel_ok = jnp.concatenate([jnp.isfinite(g_val), jnp.ones((b, h, n_q, 1), dtype=bool)], axis=-1)
        kg = kb[bi, hi, sel]
        vg = vb[bi, hi, sel]
        s = jnp.einsum('bhqd,bhqsnd->bhqsn', qc, kg).astype(F32) * scale
        kpos = sel[..., None] * MOBA_BLOCK + offs
        qpos = pc[None, None, :, None, None]
        s = s - slopes[None, :, None, None, None] * (qpos - kpos).astype(F32)
        s = jnp.where(sel_ok[..., None] & (kpos <= qpos), s, -jnp.inf)
        p = jax.nn.softmax(s.reshape(b, h, n_q, -1), axis=-1).reshape(s.shape)
        return jnp.einsum('bhqsn,bhqsnd->bhqd', p.astype(vg.dtype), vg)

    qlen = MOBA_QCHUNK if tq % MOBA_QCHUNK == 0 else tq
    nq = tq // qlen
    qs = jnp.moveaxis(q.transpose(0, 2, 1, 3).reshape(b, h, nq, qlen, dh), 2, 0)
    ps = q_pos.reshape(nq, qlen)
    o = lax.map(attend_chunk, (qs, ps))
    return jnp.moveaxis(o, 0, 2).reshape(b, h, tq, dh).transpose(0, 2, 1, 3)


def _moba_mixer(q, k, v, k_past, v_past, pos0, q_g, k_g):
    b, t, _ = q.shape
    shp = (b, t, ATTN_HEADS, ATTN_HEADDIM)
    q = _rmsnorm(q.reshape(shp), q_g)
    k = _rmsnorm(k.reshape(shp), k_g)
    v = v.reshape(shp)
    if k_past is None:
        k_all, v_all = k, v
    else:
        k_all = jnp.concatenate([k_past.astype(k.dtype), k], axis=1)
        v_all = jnp.concatenate([v_past.astype(v.dtype), v], axis=1)
    o = _moba_attend(q, k_all, v_all, pos0 + jnp.arange(t, dtype=jnp.int32))
    return o.reshape(b, t, D_GROUP), k, v


GATHER_TILE = 256


def _gather_rows_kernel(idx_ref, src_ref, o_ref, sem):
    base = pl.program_id(0) * GATHER_TILE

    def row_copy(r):
        return pltpu.make_async_copy(src_ref.at[pl.ds(idx_ref[base + r], 1), :], o_ref.at[pl.ds(r, 1), :], sem)

    def start(r, carry):
        row_copy(r).start()
        return carry

    def wait(r, carry):
        row_copy(r).wait()
        return carry

    lax.fori_loop(0, GATHER_TILE, start, 0)
    lax.fori_loop(0, GATHER_TILE, wait, 0)


def _gather_rows(src, idx):
    n_out = idx.shape[0]
    assert n_out % GATHER_TILE == 0
    cols = src.shape[1]
    return pl.pallas_call(
        _gather_rows_kernel,
        name="gather_rows",
        grid_spec=pltpu.PrefetchScalarGridSpec(
            num_scalar_prefetch=1,
            grid=(n_out // GATHER_TILE,),
            in_specs=[pl.BlockSpec(memory_space=pl.ANY)],
            out_specs=pl.BlockSpec((GATHER_TILE, cols), lambda i, idx_ref: (i, 0)),
            scratch_shapes=[pltpu.SemaphoreType.DMA(())],
        ),
        out_shape=jax.ShapeDtypeStruct((n_out, cols), src.dtype),
        compiler_params=pltpu.CompilerParams(
            dimension_semantics=("arbitrary",), vmem_limit_bytes=VMEM_LIMIT_BYTES),
    )(idx, src)


def _dense_ffn(h2, x_res, g2, rows_per_gate, w1, w3, w2, l2, *, tm):
    rows = h2.shape[0]
    n_tiles = rows // tm
    grp = jnp.full((n_tiles,), l2, jnp.int32)
    act = _gmm(h2, [w1, w3], grp, tm=tm, tn=512, out_dtype=BF16, name="ffn_up")
    w2h = w2.reshape(w2.shape[0] * 2, D_FF_EXPERT, D_MODEL)
    part = _gmm(act, [w2h], grp * 2, tm=tm, tn=512, out_dtype=F32, x_col_block=0, k_dim=D_FF_EXPERT,
                name="ffn_down_lo")
    return _gmm(act, [w2h], grp * 2 + 1, tm=tm, tn=512, out_dtype=F32, x_col_block=1, k_dim=D_FF_EXPERT,
                name="ffn_down_hi", partial=part, gate=g2, rows_per_gate=rows_per_gate, res=x_res)


def _moe_ffn(h2, logits, router_b, w1, w3, w2, l2, *, tm):
    n_tok = h2.shape[0]
    lg = logits[:, :N_EXPERTS] + router_b.astype(F32)
    top_v, top_i = lax.top_k(lg, TOP_K)
    gates = jax.nn.softmax(top_v, axis=-1)

    n_pairs = n_tok * TOP_K
    n_tiles = -(-n_pairs // tm) + N_EXPERTS
    p_rows = n_tiles * tm
    e_flat = top_i.reshape(-1).astype(jnp.int32)
    onehot = (e_flat[:, None] == jnp.arange(N_EXPERTS, dtype=jnp.int32)[None, :]).astype(jnp.int32)
    rank = jnp.sum((jnp.cumsum(onehot, axis=0) - onehot) * onehot, axis=1)
    counts = jnp.sum(onehot, axis=0)
    tiles_e = (counts + tm - 1) // tm
    tile_end = jnp.cumsum(tiles_e)
    starts = (tile_end - tiles_e) * tm
    pos = starts[e_flat] + rank
    tok = jnp.arange(n_pairs, dtype=jnp.int32) // TOP_K
    row_token = jnp.zeros((p_rows,), jnp.int32).at[pos].set(tok)
    row_gate = jnp.zeros((p_rows,), F32).at[pos].set(gates.reshape(-1))
    tile_ids = jnp.arange(n_tiles, dtype=jnp.int32)
    total_tiles = tile_end[-1]
    tile_valid = (tile_ids < total_tiles).astype(jnp.int32)
    clipped = jnp.minimum(tile_ids, total_tiles - 1)
    tile_expert = jnp.sum((clipped[:, None] >= tile_end[None, :]).astype(jnp.int32), axis=1)
    tile_group = (l2 * N_EXPERTS + tile_expert).astype(jnp.int32)

    xs = _gather_rows(h2, row_token).astype(BF16)
    n_moe = w1.shape[0]
    w1r = w1.reshape(n_moe * N_EXPERTS, D_MODEL, D_FF_EXPERT)
    w3r = w3.reshape(n_moe * N_EXPERTS, D_MODEL, D_FF_EXPERT)
    w2r = w2.reshape(n_moe * N_EXPERTS, D_FF_EXPERT, D_MODEL)
    act = _gmm(xs, [w1r, w3r], tile_group, tm=tm, tn=512, out_dtype=BF16, name="moe_up", tile_valid=tile_valid)
    ys = _gmm(act, [w2r], tile_group, tm=tm, tn=512, out_dtype=F32, name="moe_down",
              rowscale=row_gate.reshape(p_rows, 1), tile_valid=tile_valid)
    pos2 = pos.reshape(n_tok, TOP_K)
    n_pad = -(-n_tok // GATHER_TILE) * GATHER_TILE
    pos2 = jnp.pad(pos2, ((0, n_pad - n_tok), (0, 0)))
    return _gather_rows(ys, pos2[:, 0])[:n_tok] + _gather_rows(ys, pos2[:, 1])[:n_tok]


def _layer(l, xr, b, t, mod, pos0, p, cache, page_table):
    rows = b * t
    tm = 512 if rows % 512 == 0 else rows
    tr = 256 if rows % 256 == 0 else rows
    n_tiles = rows // tm
    if True:
        if cache is None:
            k_past = v_past = None
        else:
            cache_k, cache_v, st_ssd, st_conv, st_pool, st_re, st_im = cache
            n_pool = cache_k.shape[1]
            pages = page_table + l * n_pool
            k_past = cache_k.reshape((DEPTH * n_pool,) + cache_k.shape[2:])[pages]
            v_past = cache_v.reshape((DEPTH * n_pool,) + cache_v.shape[2:])[pages]
            k_past = k_past.reshape(b, -1, ATTN_HEADS, ATTN_HEADDIM)
            v_past = v_past.reshape(b, -1, ATTN_HEADS, ATTN_HEADDIM)
            ssd_h0, conv_buf, pool_buf = st_ssd[l], st_conv[l], st_pool[l]
            s5_h0_re, s5_h0_im = st_re[l], st_im[l]
        sh1, sc1, g1, sh2, sc2, g2 = [m.reshape(b, 1, D_MODEL) for m in jnp.split(mod, 6, axis=-1)]
        if t >= tm:
            gate_rows = t
            as_gate = lambda m: m
        else:
            gate_rows = rows
            as_gate = lambda m: jnp.broadcast_to(m, (b, t, D_MODEL)).reshape(1, rows, D_MODEL)
        grp = jnp.full((n_tiles,), l, jnp.int32)

        h = _norm_mod(xr, p['norm1_g'][l], as_gate(sc1), as_gate(sh1), tr=tr, rows_per_gate=gate_rows)
        proj_a = _gmm(h, [p['w_in']], grp, tm=tm, tn=512, out_dtype=F32, n_cols=OFF_DT, name="proj_a")
        proj_b = _gmm(h, [p['w_in_b']], grp, tm=tm, tn=512, out_dtype=F32, name="proj_b")
        proj_dt = _gmm(h, [p['w_in_dt']], grp, tm=tm, tn=LANES, out_dtype=F32, name="proj_dt")
        proj_a = proj_a.reshape(b, t, OFF_DT)
        proj_b = proj_b.reshape(b, t, D_IN - OFF_POOL)
        proj_dt = proj_dt.reshape(b, t, LANES)
        seg = lambda i: proj_b[..., i * D_GROUP:(i + 1) * D_GROUP]
        to_rows = lambda v: v.reshape(rows, D_GROUP).astype(BF16)

        if cache is None:
            y_ssdn, ssd_new = _ssd_prompt(proj_a, proj_dt, p, l)
            conv_new = proj_a[:, t - (SSD_CONV - 1):, OFF_XBC:OFF_DT]
            y_pooln = _pool_prompt(proj_b, p, l, col0=0)
            pool_new = proj_b[:, t - POOL_BUF:, :D_GROUP]
            y_s5n, re_new, im_new = _s5_prompt(proj_b, p, l, col0=D_GROUP // S5_CH_BLOCK)
            y_att, k_new = _moba_prompt(proj_b, p['attn_q_g'][l], p['attn_k_g'][l], col0=2)
            k_new = k_new.reshape(b, t, ATTN_HEADS, ATTN_HEADDIM)
            v_new = seg(4).reshape(b, t, ATTN_HEADS, ATTN_HEADDIM)
        else:
            y_ssd, conv_new, ssd_new = _ssd_mixer(
                proj_a[..., OFF_Z:OFF_XBC], proj_a[..., OFF_XBC:OFF_DT], proj_dt[..., :SSD_HEADS],
                conv_buf, ssd_h0, p['ssd_conv_w'][l], p['ssd_conv_b'][l], p['ssd_dt_bias'][l],
                p['ssd_a_log'][l], p['ssd_d'][l], p['ssd_norm_g'][l])
            y_ssdn = to_rows(y_ssd)
            y_pool, pool_new = _pool_mixer(seg(0), pool_buf, pos0, p['pool_w'][l], p['pool_scale'][l])
            y_pooln = to_rows(_rmsnorm(y_pool, p['pool_norm_g'][l]))
            y_s5, re_new, im_new = _s5_mixer(
                seg(1), s5_h0_re, s5_h0_im, p['s5_a_re'][l], p['s5_a_im'][l],
                p['s5_log_dt'][l], p['s5_b_re'][l], p['s5_b_im'][l], p['s5_c_re'][l], p['s5_c_im'][l],
                p['s5_d'][l], p['s5_glu_w'][l], p['s5_glu_b'][l])
            y_s5n = _rmsnorm(y_s5, p['s5_norm_g'][l]).reshape(rows, D_GROUP).astype(BF16)
            y_att, k_new, v_new = _moba_mixer(seg(2), seg(3), seg(4), k_past, v_past, pos0,
                                              p['attn_q_g'][l], p['attn_k_g'][l])
        mixed_in = jnp.concatenate([y_ssdn,
                                    y_pooln,
                                    y_s5n,
                                    to_rows(_rmsnorm(y_att, p['attn_norm_g'][l]))], axis=-1)
        xr = _gmm(mixed_in, [p['w_out']], grp, tm=tm, tn=512, out_dtype=F32, name="w_out",
                  gate=as_gate(g1), rows_per_gate=gate_rows, res=xr)

        states = (k_new, v_new, ssd_new, conv_new, pool_new, re_new, im_new)
        if l % 2 == 0:
            h2 = _norm_mod(xr, p['norm2_g'][l], as_gate(sc2), as_gate(sh2), tr=tr, rows_per_gate=gate_rows)
            xr = _dense_ffn(h2, xr, as_gate(g2), gate_rows, p['ffn_w1'], p['ffn_w3'], p['ffn_w2'], l // 2, tm=tm)
            return xr, states, None
        h2, logits = _norm_mod(xr, p['norm2_g'][l], as_gate(sc2), as_gate(sh2), tr=tr,
                               rows_per_gate=gate_rows, router_w=p['moe_router_w'][l // 2])
        g2_rows = jnp.broadcast_to(g2, (b, t, D_MODEL)).reshape(rows, D_MODEL)
        return xr, states, (h2, logits, g2_rows)


def kernel(x_prompt, x_sample, cache_k, cache_v, state_ssd, state_ssd_conv, state_pool,
           state_s5_re, state_s5_im, page_table, c_prompt, c_sample,
           ada_w, ada_b, norm1_g, norm2_g, w_in, w_out,
           ssd_conv_w, ssd_conv_b, ssd_dt_bias, ssd_a_log, ssd_d, ssd_norm_g,
           pool_w, pool_scale, pool_norm_g,
           s5_a_re, s5_a_im, s5_log_dt, s5_b_re, s5_b_im, s5_c_re, s5_c_im,
           s5_d, s5_glu_w, s5_glu_b, s5_norm_g,
           attn_q_g, attn_k_g, attn_norm_g,
           ffn_w1, ffn_w3, ffn_w2,
           moe_router_w, moe_router_b, moe_w1, moe_w3, moe_w2):
    w_in_b = w_in[:, :, OFF_POOL:]
    w_in_dt = jnp.pad(w_in[:, :, OFF_DT:OFF_POOL], ((0, 0), (0, 0), (0, LANES - SSD_HEADS)))
    p = dict(norm1_g=norm1_g, norm2_g=norm2_g, w_in=w_in, w_in_b=w_in_b, w_in_dt=w_in_dt, w_out=w_out,
             ssd_conv_w=ssd_conv_w, ssd_conv_b=ssd_conv_b, ssd_dt_bias=ssd_dt_bias,
             ssd_a_log=ssd_a_log, ssd_d=ssd_d, ssd_norm_g=ssd_norm_g,
             pool_w=pool_w, pool_scale=pool_scale, pool_norm_g=pool_norm_g,
             s5_a_re=s5_a_re, s5_a_im=s5_a_im, s5_log_dt=s5_log_dt, s5_b_re=s5_b_re, s5_b_im=s5_b_im,
             s5_c_re=s5_c_re, s5_c_im=s5_c_im, s5_d=s5_d, s5_glu_w=s5_glu_w, s5_glu_b=s5_glu_b,
             s5_norm_g=s5_norm_g, attn_q_g=attn_q_g, attn_k_g=attn_k_g, attn_norm_g=attn_norm_g,
             ffn_w1=ffn_w1, ffn_w3=ffn_w3, ffn_w2=ffn_w2,
             moe_router_w=moe_router_w, moe_router_b=moe_router_b,
             moe_w1=moe_w1, moe_w3=moe_w3, moe_w2=moe_w2)

    c_all = jnp.concatenate([c_prompt, c_sample], axis=0)
    n_c = c_all.shape[0]
    c_pad = jnp.pad(jax.nn.silu(c_all), ((0, BF16_SUBLANES - n_c), (0, 0))).astype(BF16)
    mods_p, mods_s = [], []
    for l in range(DEPTH):
        m = _gmm(c_pad, [ada_w], jnp.full((1,), l, jnp.int32), tm=BF16_SUBLANES, tn=512, out_dtype=F32,
                 name="ada_mod")
        m = m[:n_c] + ada_b[l]
        mods_p.append(m[:BATCH])
        mods_s.append(m[BATCH:])

    cache = (cache_k, cache_v, state_ssd, state_ssd_conv, state_pool, state_s5_re, state_s5_im)
    n_p, n_s = BATCH * SEQ, DEC_BATCH * DEC_SEQ
    xp = x_prompt.reshape(n_p, D_MODEL)
    xs = x_sample.reshape(n_s, D_MODEL)
    new_p, new_s = [[] for _ in range(7)], [[] for _ in range(7)]
    for l in range(DEPTH):
        xp, st_p, moe_p = _layer(l, xp, BATCH, SEQ, mods_p[l], 0, p, None, None)
        xs, st_s, moe_s = _layer(l, xs, DEC_BATCH, DEC_SEQ, mods_s[l], PAST_LEN, p, cache, page_table)
        if moe_p is not None:
            h2 = jnp.concatenate([moe_p[0], moe_s[0]], axis=0)
            logits = jnp.concatenate([moe_p[1], moe_s[1]], axis=0)
            f = _moe_ffn(h2, logits, moe_router_b[l // 2], moe_w1, moe_w3, moe_w2, l // 2, tm=512)
            xp = xp + moe_p[2] * f[:n_p]
            xs = xs + moe_s[2] * f[n_p:]
        for lst, val in zip(new_p, st_p):
            lst.append(val)
        for lst, val in zip(new_s, st_s):
            lst.append(val)
    y_prompt = xp.reshape(BATCH, SEQ, D_MODEL)
    y_sample = xs.reshape(DEC_BATCH, DEC_SEQ, D_MODEL)
    k_p, v_p, ssd_p, conv_p, pool_p, re_p, im_p = [jnp.stack(v, axis=0) for v in new_p]
    k_s, v_s, ssd_s, conv_s, pool_s, re_s, im_s = [jnp.stack(v, axis=0) for v in new_s]
    return (y_prompt, y_sample, k_p, v_p, ssd_p, conv_p, pool_p, re_p, im_p,
            k_s, v_s, ssd_s, conv_s, pool_s, re_s, im_s)
```

```python
import functools
import math

import jax
import jax.numpy as jnp
from jax import lax
from jax.experimental import pallas as pl
from jax.experimental.pallas import tpu as pltpu

D_MODEL = 4096
BATCH = 4
SEQ = 2048
DEPTH = 4
DEC_BATCH = 8
DEC_SEQ = 4
PAST_LEN = 8192
PAGE_SIZE = 128
D_GROUP = D_MODEL // 4
SSD_HEADDIM = 64
SSD_HEADS = D_GROUP // SSD_HEADDIM
SSD_STATE = 128
SSD_GROUPS = 2
SSD_CONV = 4
SSD_CHUNK = 128
SSD_CONV_DIM = D_GROUP + 2 * SSD_GROUPS * SSD_STATE
POOL_WINDOWS = (2, 4, 8, 16)
POOL_GROUP = D_GROUP // len(POOL_WINDOWS)
POOL_BUF = max(POOL_WINDOWS) - 1
S5_CH = 16
S5_GROUPS = D_GROUP // S5_CH
S5_STATE = 64
ATTN_HEADDIM = 128
ATTN_HEADS = D_GROUP // ATTN_HEADDIM
MOBA_BLOCK = 256
MOBA_TOPK = 3
MOBA_QCHUNK = 16
D_FF = 11008
N_EXPERTS = 8
TOP_K = 2
D_FF_EXPERT = D_FF // 2
EPS = 1e-6
OFF_Z = 0
OFF_XBC = OFF_Z + D_GROUP
OFF_DT = OFF_XBC + SSD_CONV_DIM
OFF_POOL = OFF_DT + SSD_HEADS
OFF_S5 = OFF_POOL + D_GROUP
OFF_Q = OFF_S5 + D_GROUP
OFF_K = OFF_Q + D_GROUP
OFF_V = OFF_K + D_GROUP
D_IN = OFF_V + D_GROUP

F32 = jnp.float32
BF16 = jnp.bfloat16
LANES = 128
BF16_SUBLANES = 16
VMEM_LIMIT_BYTES = 56 * 1024 * 1024


def _gmm_kernel(*refs, n_w, has_res, has_gate, has_partial, has_rowscale, has_valid):
    it = iter(refs)
    te_ref, tv_ref, x_ref = next(it), next(it), next(it)
    w_refs = [next(it) for _ in range(n_w)]
    res_ref = next(it) if has_res else None
    gate_ref = next(it) if has_gate else None
    partial_ref = next(it) if has_partial else None
    rowscale_ref = next(it) if has_rowscale else None
    o_ref = next(it)
    wbf_refs = [next(it) for _ in range(n_w)]

    i = pl.program_id(1)
    prev = te_ref[jnp.maximum(i - 1, 0)]
    group_changed = jnp.logical_or(i == 0, te_ref[i] != prev)

    @pl.when(group_changed)
    def _():
        for w_ref, wbf_ref in zip(w_refs, wbf_refs):
            wbf_ref[...] = w_ref[0].astype(BF16)

    def compute():
        x = x_ref[...]
        acc = jnp.dot(x, wbf_refs[0][...], preferred_element_type=F32)
        if n_w == 2:
            up = jnp.dot(x, wbf_refs[1][...], preferred_element_type=F32)
            acc = (acc * (1.0 / (1.0 + jnp.exp(-acc)))) * up
        if has_partial:
            acc = partial_ref[...] + acc
        if has_rowscale:
            acc = rowscale_ref[...] * acc
        if has_gate:
            acc = gate_ref[0] * acc
        if has_res:
            acc = res_ref[...] + acc
        o_ref[...] = acc.astype(o_ref.dtype)

    if has_valid:
        pl.when(tv_ref[i] != 0)(compute)

        @pl.when(tv_ref[i] == 0)
        def _():
            o_ref[...] = jnp.zeros_like(o_ref)
    else:
        compute()


def _gmm(x, ws, tile_group, *, tm, tn, out_dtype, name, n_cols=None, x_col_block=0, k_dim=None,
         res=None, gate=None, rows_per_gate=None, partial=None, rowscale=None, tile_valid=None):
    p_rows = x.shape[0]
    k = ws[0].shape[1] if k_dim is None else k_dim
    n_cols = ws[0].shape[2] if n_cols is None else n_cols
    n_row_tiles = p_rows // tm
    assert n_row_tiles * tm == p_rows and tile_group.shape == (n_row_tiles,)
    has_valid = tile_valid is not None
    if tile_valid is None:
        tile_valid = jnp.ones((n_row_tiles,), jnp.int32)
    grid = (pl.cdiv(n_cols, tn), n_row_tiles)

    in_specs = [pl.BlockSpec((tm, k), lambda j, i, te, tv: (i, x_col_block))]
    args = [x]
    for w in ws:
        in_specs.append(pl.BlockSpec((1, k, tn), lambda j, i, te, tv: (te[i], 0, j)))
        args.append(w)
    if res is not None:
        in_specs.append(pl.BlockSpec((tm, tn), lambda j, i, te, tv: (i, j)))
        args.append(res)
    if gate is not None:
        rg = gate.shape[1]
        in_specs.append(pl.BlockSpec((1, rg, tn), lambda j, i, te, tv: ((i * tm) // rows_per_gate, 0, j)))
        args.append(gate)
    if partial is not None:
        in_specs.append(pl.BlockSpec((tm, tn), lambda j, i, te, tv: (i, j)))
        args.append(partial)
    if rowscale is not None:
        in_specs.append(pl.BlockSpec((tm, 1), lambda j, i, te, tv: (i, 0)))
        args.append(rowscale)

    kernel = functools.partial(
        _gmm_kernel, n_w=len(ws), has_res=res is not None, has_gate=gate is not None,
        has_partial=partial is not None, has_rowscale=rowscale is not None, has_valid=has_valid)
    return pl.pallas_call(
        kernel,
        name=name,
        grid_spec=pltpu.PrefetchScalarGridSpec(
            num_scalar_prefetch=2,
            grid=grid,
            in_specs=in_specs,
            out_specs=pl.BlockSpec((tm, tn), lambda j, i, te, tv: (i, j)),
            scratch_shapes=[pltpu.VMEM((k, tn), BF16) for _ in ws],
        ),
        out_shape=jax.ShapeDtypeStruct((p_rows, n_cols), out_dtype),
        compiler_params=pltpu.CompilerParams(
            dimension_semantics=("arbitrary", "arbitrary"),
            vmem_limit_bytes=VMEM_LIMIT_BYTES),
    )(tile_group, tile_valid, *args)


def _norm_mod_kernel(*refs, has_router):
    if has_router:
        x_ref, g_ref, sc_ref, sh_ref, rw_ref, h_ref, logit_ref = refs
    else:
        x_ref, g_ref, sc_ref, sh_ref, h_ref = refs
    xf = x_ref[...]
    ms = jnp.mean(xf * xf, axis=-1, keepdims=True)
    y = xf * lax.rsqrt(ms + EPS) * g_ref[...]
    h = y * (1.0 + sc_ref[0]) + sh_ref[0]
    h_ref[...] = h.astype(h_ref.dtype)
    if has_router:
        logit_ref[...] = jnp.dot(h, rw_ref[...], preferred_element_type=F32,
                                 precision=lax.Precision.HIGHEST)


def _norm_mod(x, g, sc, sh, *, tr, rows_per_gate, router_w=None):
    r, d = x.shape
    rg = sc.shape[1]
    mod_spec = pl.BlockSpec((1, rg, d), lambda i: ((i * tr) // rows_per_gate, 0, 0))
    in_specs = [pl.BlockSpec((tr, d), lambda i: (i, 0)), pl.BlockSpec((1, d), lambda i: (0, 0)),
                mod_spec, mod_spec]
    args = [x, g.reshape(1, d), sc, sh]
    out_shape = [jax.ShapeDtypeStruct((r, d), BF16 if router_w is None else F32)]
    out_specs = [pl.BlockSpec((tr, d), lambda i: (i, 0))]
    if router_w is not None:
        rw = jnp.pad(router_w, ((0, 0), (0, LANES - router_w.shape[1])))
        in_specs.append(pl.BlockSpec((d, LANES), lambda i: (0, 0)))
        args.append(rw)
        out_shape.append(jax.ShapeDtypeStruct((r, LANES), F32))
        out_specs.append(pl.BlockSpec((tr, LANES), lambda i: (i, 0)))
    out = pl.pallas_call(
        functools.partial(_norm_mod_kernel, has_router=router_w is not None),
        name="norm_mod",
        grid=(r // tr,),
        in_specs=in_specs,
        out_specs=out_specs,
        out_shape=out_shape,
        compiler_params=pltpu.CompilerParams(
            dimension_semantics=("arbitrary",), vmem_limit_bytes=VMEM_LIMIT_BYTES),
    )(*args)
    return out if router_w is not None else out[0]


def _qk_norm_kernel(q_ref, k_ref, v_ref, qg_ref, kg_ref, qn_ref, kn_ref, knb_ref, vb_ref, km_ref):
    for hd in range(ATTN_HEADS):
        sl = slice(hd * ATTN_HEADDIM, (hd + 1) * ATTN_HEADDIM)
        q = q_ref[0, :, sl]
        k = k_ref[0, :, sl]
        qn = q * lax.rsqrt(jnp.mean(q * q, axis=-1, keepdims=True) + EPS) * qg_ref[...]
        kn = k * lax.rsqrt(jnp.mean(k * k, axis=-1, keepdims=True) + EPS) * kg_ref[...]
        qn_ref[0, :, sl] = qn
        kn_ref[0, :, sl] = kn
        knb_ref[0, :, sl] = kn.astype(BF16)
        km_ref[0, :, sl] = jnp.mean(kn, axis=0, keepdims=True)
    vb_ref[0] = v_ref[0].astype(BF16)


def _moba_kernel(slopes_ref, q_ref, k_ref, v_ref, km_ref, o_ref):
    hd = pl.program_id(1)
    n = pl.program_id(2)
    blk_rows = MOBA_BLOCK
    q = q_ref[0]
    km = km_ref[0]
    gate_t = lax.dot_general(km, q, (((1,), (1,)), ((), ())), preferred_element_type=F32,
                             precision=lax.Precision.HIGHEST)
    blk_t = lax.broadcasted_iota(jnp.int32, gate_t.shape, 0)
    past_t = blk_t < n
    gm = jnp.where(past_t, gate_t, -jnp.inf)
    rank = jnp.zeros(gate_t.shape, jnp.int32)
    for m in range(gate_t.shape[0]):
        row_m = gm[m:m + 1, :]
        beats = jnp.logical_or(row_m > gm, jnp.logical_and(row_m == gm, m < blk_t))
        rank = rank + beats.astype(jnp.int32)
    sel_t = jnp.logical_and(past_t, rank < MOBA_TOPK).astype(BF16)

    qb = q.astype(BF16)
    slope = slopes_ref[hd]
    scale = 1.0 / math.sqrt(ATTN_HEADDIM)
    r_id = lax.broadcasted_iota(jnp.int32, (blk_rows, blk_rows), 0)
    c_id = lax.broadcasted_iota(jnp.int32, (blk_rows, blk_rows), 1)
    rc = (r_id - c_id).astype(F32)
    eye = (r_id == c_id).astype(BF16)
    sel = lax.dot_general(eye, sel_t, (((1,), (1,)), ((), ())), preferred_element_type=F32)
    blk_id = lax.broadcasted_iota(jnp.int32, sel.shape, 1)

    def scores(j):
        kj = k_ref[0, pl.ds(pl.multiple_of(j * blk_rows, blk_rows), blk_rows), :]
        return lax.dot_general(qb, kj, (((1,), (1,)), ((), ())), preferred_element_type=F32) * scale

    def values(j):
        return v_ref[0, pl.ds(pl.multiple_of(j * blk_rows, blk_rows), blk_rows), :]

    s = scores(n) - slope * rc
    s = jnp.where(c_id <= r_id, s, -jnp.inf)
    m0 = jnp.max(s, axis=-1, keepdims=True)
    p = jnp.exp(s - m0)
    l0 = jnp.sum(p, axis=-1, keepdims=True)
    acc0 = jnp.dot(p.astype(BF16), values(n), preferred_element_type=F32)

    two = 2 * blk_rows
    c2_id = lax.broadcasted_iota(jnp.int32, (blk_rows, two), 1)
    rc2 = (lax.broadcasted_iota(jnp.int32, (blk_rows, two), 0) - c2_id).astype(F32)
    second = c2_id >= blk_rows

    def body(i, carry):
        m_run, l_run, acc = carry
        j = 2 * i
        rows = pl.ds(pl.multiple_of(j * blk_rows, two), two)
        kj = k_ref[0, rows, :]
        sj = lax.dot_general(qb, kj, (((1,), (1,)), ((), ())), preferred_element_type=F32) * scale
        sj = sj - slope * (rc2 + ((n - j) * blk_rows).astype(F32))
        sel_a = jnp.sum(jnp.where(blk_id == j, sel, 0.0), axis=-1, keepdims=True)
        sel_b = jnp.sum(jnp.where(blk_id == j + 1, sel, 0.0), axis=-1, keepdims=True)
        sj = jnp.where(jnp.where(second, sel_b, sel_a) > 0.5, sj, -jnp.inf)
        m_new = jnp.maximum(m_run, jnp.max(sj, axis=-1, keepdims=True))
        alpha = jnp.exp(m_run - m_new)
        pj = jnp.exp(sj - m_new)
        l_new = alpha * l_run + jnp.sum(pj, axis=-1, keepdims=True)
        acc = alpha * acc + jnp.dot(pj.astype(BF16), v_ref[0, rows, :], preferred_element_type=F32)
        return m_new, l_new, acc

    _, l_fin, acc = lax.fori_loop(0, (n + 1) // 2, body, (m0, l0, acc0))
    o_ref[0] = acc / l_fin


def _moba_prompt(proj, q_g, k_g, *, col0):
    b, t, _ = proj.shape
    nblk = t // MOBA_BLOCK
    assert nblk * MOBA_BLOCK == t
    row_spec = lambda c: pl.BlockSpec((1, MOBA_BLOCK, D_GROUP), lambda bi, ni, c=c: (bi, ni, c))
    g_spec = pl.BlockSpec((1, ATTN_HEADDIM), lambda bi, ni: (0, 0))
    out_spec = pl.BlockSpec((1, MOBA_BLOCK, D_GROUP), lambda bi, ni: (bi, ni, 0))
    qn, kn, knb, vb, km = pl.pallas_call(
        _qk_norm_kernel,
        name="qk_norm",
        grid=(b, nblk),
        in_specs=[row_spec(col0), row_spec(col0 + 1), row_spec(col0 + 2), g_spec, g_spec],
        out_specs=[out_spec, out_spec, out_spec, out_spec,
                   pl.BlockSpec((1, 1, D_GROUP), lambda bi, ni: (bi * nblk + ni, 0, 0))],
        out_shape=[jax.ShapeDtypeStruct((b, t, D_GROUP), F32), jax.ShapeDtypeStruct((b, t, D_GROUP), F32),
                   jax.ShapeDtypeStruct((b, t, D_GROUP), BF16), jax.ShapeDtypeStruct((b, t, D_GROUP), BF16),
                   jax.ShapeDtypeStruct((b * nblk, 1, D_GROUP), F32)],
        compiler_params=pltpu.CompilerParams(
            dimension_semantics=("arbitrary", "arbitrary"), vmem_limit_bytes=VMEM_LIMIT_BYTES),
    )(proj, proj, proj, q_g.reshape(1, ATTN_HEADDIM), k_g.reshape(1, ATTN_HEADDIM))
    km = km.reshape(b, nblk, D_GROUP)

    slopes = _alibi_slopes(ATTN_HEADS)
    o = pl.pallas_call(
        _moba_kernel,
        name="moba_attend",
        grid_spec=pltpu.PrefetchScalarGridSpec(
            num_scalar_prefetch=1,
            grid=(b, ATTN_HEADS, nblk),
            in_specs=[
                pl.BlockSpec((1, MOBA_BLOCK, ATTN_HEADDIM), lambda bi, hi, ni, sl: (bi, ni, hi)),
                pl.BlockSpec((1, t, ATTN_HEADDIM), lambda bi, hi, ni, sl: (bi, 0, hi)),
                pl.BlockSpec((1, t, ATTN_HEADDIM), lambda bi, hi, ni, sl: (bi, 0, hi)),
                pl.BlockSpec((1, nblk, ATTN_HEADDIM), lambda bi, hi, ni, sl: (bi, 0, hi)),
            ],
            out_specs=pl.BlockSpec((1, MOBA_BLOCK, ATTN_HEADDIM), lambda bi, hi, ni, sl: (bi, ni, hi)),
        ),
        out_shape=jax.ShapeDtypeStruct((b, t, D_GROUP), F32),
        compiler_params=pltpu.CompilerParams(
            dimension_semantics=("arbitrary", "arbitrary", "arbitrary"), vmem_limit_bytes=VMEM_LIMIT_BYTES),
    )(slopes, qn, knb, vb, km)
    return o, kn


S5_LANE_BLOCK = 512
S5_CH_BLOCK = S5_LANE_BLOCK // S5_STATE * S5_CH
S5_N_BLOCKS = D_GROUP // S5_CH_BLOCK
S5_ROWS = 8
S5_MM_ROWS = 256


def _gelu_tanh(x):
    return 0.5 * x * (1.0 + jnp.tanh(math.sqrt(2.0 / math.pi) * (x + 0.044715 * (x * x * x))))


def _s5_kernel(u_ref, win_ref, tab_ref, wc_ref, d_ref, y_ref, hre_ref, him_ref, bre, bim, *, t_len):
    nl = S5_LANE_BLOCK

    def in_mm(c, carry):
        rows = pl.ds(pl.multiple_of(c * S5_MM_ROWS, S5_MM_ROWS), S5_MM_ROWS)
        r = jnp.dot(u_ref[0, rows, :].astype(BF16), win_ref[0], preferred_element_type=F32)
        bre[rows, :] = r[:, :nl]
        bim[rows, :] = r[:, nl:]
        return carry

    lax.fori_loop(0, t_len // S5_MM_ROWS, in_mm, 0)

    pre = tab_ref[0, 0:8, :]
    pim = tab_ref[0, 8:16, :]
    row = lax.broadcasted_iota(jnp.int32, (S5_ROWS, nl), 0)
    steps = [(s, jnp.where(row >= s, tab_ref[0, i:i + 1, :], 0.0), jnp.where(row >= s, tab_ref[0, i + 1:i + 2, :], 0.0))
             for s, i in ((1, 16), (2, 18), (4, 20))]

    def scan_rows(i, carry):
        cr, ci = carry
        rows = pl.ds(pl.multiple_of(i * S5_ROWS, S5_ROWS), S5_ROWS)
        xr = bre[rows, :]
        xi = bim[rows, :]
        for s, ar, ai in steps:
            sr = pltpu.roll(xr, s, 0)
            si = pltpu.roll(xi, s, 0)
            xr, xi = xr + (ar * sr - ai * si), xi + (ar * si + ai * sr)
        xr, xi = xr + (pre * cr - pim * ci), xi + (pre * ci + pim * cr)
        bre[rows, :] = xr
        bim[rows, :] = xi
        return (jnp.broadcast_to(xr[S5_ROWS - 1:S5_ROWS, :], (S5_ROWS, nl)),
                jnp.broadcast_to(xi[S5_ROWS - 1:S5_ROWS, :], (S5_ROWS, nl)))

    zero = jnp.zeros((S5_ROWS, nl), F32)
    cr, ci = lax.fori_loop(0, t_len // S5_ROWS, scan_rows, (zero, zero), unroll=2)
    hre_ref[0] = cr[0:1, :]
    him_ref[0] = ci[0:1, :]

    def out_mm(c, carry):
        rows = pl.ds(pl.multiple_of(c * S5_MM_ROWS, S5_MM_ROWS), S5_MM_ROWS)
        hcat = jnp.concatenate([bre[rows, :], bim[rows, :]], axis=1).astype(BF16)
        y = jnp.dot(hcat, wc_ref[0], preferred_element_type=F32)
        y = y + d_ref[...] * u_ref[0, rows, :]
        y_ref[0, rows, :] = _gelu_tanh(y)
        return carry

    lax.fori_loop(0, t_len // S5_MM_ROWS, out_mm, 0)


def _s5_tables(a_re, a_im, log_dt, b_re, b_im, c_re, c_im):
    dt = jnp.exp(log_dt.astype(F32))[:, None]
    ar, ai = a_re.astype(F32), a_im.astype(F32)
    mag = jnp.exp(dt * ar)
    abar_re, abar_im = mag * jnp.cos(dt * ai), mag * jnp.sin(dt * ai)
    den = ar * ar + ai * ai
    nr, ni = abar_re - 1.0, abar_im
    f_re, f_im = (nr * ar + ni * ai) / den, (ni * ar - nr * ai) / den
    bb_re = f_re[..., None] * b_re - f_im[..., None] * b_im
    bb_im = f_re[..., None] * b_im + f_im[..., None] * b_re

    def cmul(x, y):
        return x[0] * y[0] - x[1] * y[1], x[0] * y[1] + x[1] * y[0]

    a1 = (abar_re.reshape(-1), abar_im.reshape(-1))
    pows = [a1]
    for _ in range(S5_ROWS - 1):
        pows.append(cmul(pows[-1], a1))
    a2 = pows[1]
    a4 = pows[3]
    rows = ([p[0] for p in pows] + [p[1] for p in pows] +
            [a1[0], a1[1], a2[0], a2[1], a4[0], a4[1]] + [jnp.zeros_like(a1[0])] * 2)
    tab = jnp.stack(rows, axis=0)
    tab = tab.reshape(24, S5_N_BLOCKS, S5_LANE_BLOCK).transpose(1, 0, 2)

    gpb = S5_LANE_BLOCK // S5_STATE
    eye = jnp.eye(gpb, dtype=F32)

    def pack_in(bb):
        x = bb.reshape(S5_N_BLOCKS, gpb, S5_STATE, S5_CH)
        x = jnp.einsum('kgnc,gh->kgchn', x, eye)
        return x.reshape(S5_N_BLOCKS, gpb * S5_CH, gpb * S5_STATE)

    def pack_out(cc):
        x = cc.reshape(S5_N_BLOCKS, gpb, S5_CH, S5_STATE)
        x = jnp.einsum('kgcn,gh->kgnhc', x, eye)
        return x.reshape(S5_N_BLOCKS, gpb * S5_STATE, gpb * S5_CH)

    w_in = jnp.concatenate([pack_in(bb_re), pack_in(bb_im)], axis=2).astype(BF16)
    w_c = jnp.concatenate([pack_out(c_re.astype(F32)), -pack_out(c_im.astype(F32))], axis=1).astype(BF16)
    return tab, w_in, w_c


def _s5_post_kernel(y_ref, w_ref, b_ref, g_ref, o_ref, wbf_ref):
    @pl.when(pl.program_id(0) == 0)
    def _():
        wbf_ref[...] = w_ref[...].astype(BF16)

    y = y_ref[...]
    z = jnp.dot(y.astype(BF16), wbf_ref[...], preferred_element_type=F32) + b_ref[...]
    y = y * (1.0 / (1.0 + jnp.exp(-z)))
    y = y * lax.rsqrt(jnp.mean(y * y, axis=-1, keepdims=True) + EPS) * g_ref[...]
    o_ref[...] = y.astype(o_ref.dtype)


def _s5_prompt(proj, p, l, *, col0):
    b, t, _ = proj.shape
    tab, w_in, w_c = _s5_tables(p['s5_a_re'][l], p['s5_a_im'][l], p['s5_log_dt'][l],
                                p['s5_b_re'][l], p['s5_b_im'][l], p['s5_c_re'][l], p['s5_c_im'][l])
    nl = S5_LANE_BLOCK
    y, hre, him = pl.pallas_call(
        functools.partial(_s5_kernel, t_len=t),
        name="s5_scan",
        grid=(b, S5_N_BLOCKS),
        in_specs=[
            pl.BlockSpec((1, t, S5_CH_BLOCK), lambda bi, ki: (bi, 0, col0 + ki)),
            pl.BlockSpec((1, S5_CH_BLOCK, 2 * nl), lambda bi, ki: (ki, 0, 0)),
            pl.BlockSpec((1, 24, nl), lambda bi, ki: (ki, 0, 0)),
            pl.BlockSpec((1, 2 * nl, S5_CH_BLOCK), lambda bi, ki: (ki, 0, 0)),
            pl.BlockSpec((1, S5_CH_BLOCK), lambda bi, ki: (0, ki)),
        ],
        out_specs=[
            pl.BlockSpec((1, t, S5_CH_BLOCK), lambda bi, ki: (bi, 0, ki)),
            pl.BlockSpec((1, 1, nl), lambda bi, ki: (bi, 0, ki)),
            pl.BlockSpec((1, 1, nl), lambda bi, ki: (bi, 0, ki)),
        ],
        out_shape=[jax.ShapeDtypeStruct((b, t, D_GROUP), F32),
                   jax.ShapeDtypeStruct((b, 1, S5_GROUPS * S5_STATE), F32),
                   jax.ShapeDtypeStruct((b, 1, S5_GROUPS * S5_STATE), F32)],
        scratch_shapes=[pltpu.VMEM((t, nl), F32), pltpu.VMEM((t, nl), F32)],
        compiler_params=pltpu.CompilerParams(
            dimension_semantics=("arbitrary", "arbitrary"), vmem_limit_bytes=VMEM_LIMIT_BYTES),
    )(proj, w_in, tab, w_c, p['s5_d'][l].reshape(1, D_GROUP))

    rows = b * t
    tr = 256
    yn = pl.pallas_call(
        _s5_post_kernel,
        name="s5_post",
        grid=(rows // tr,),
        in_specs=[pl.BlockSpec((tr, D_GROUP), lambda i: (i, 0)),
                  pl.BlockSpec((D_GROUP, D_GROUP), lambda i: (0, 0)),
                  pl.BlockSpec((1, D_GROUP), lambda i: (0, 0)),
                  pl.BlockSpec((1, D_GROUP), lambda i: (0, 0))],
        out_specs=pl.BlockSpec((tr, D_GROUP), lambda i: (i, 0)),
        out_shape=jax.ShapeDtypeStruct((rows, D_GROUP), BF16),
        scratch_shapes=[pltpu.VMEM((D_GROUP, D_GROUP), BF16)],
        compiler_params=pltpu.CompilerParams(
            dimension_semantics=("arbitrary",), vmem_limit_bytes=VMEM_LIMIT_BYTES),
    )(y.reshape(rows, D_GROUP), p['s5_glu_w'][l], p['s5_glu_b'][l].reshape(1, D_GROUP),
      p['s5_norm_g'][l].reshape(1, D_GROUP))
    return yn, hre.reshape(b, S5_GROUPS, S5_STATE), him.reshape(b, S5_GROUPS, S5_STATE)


SSD_PAIRS = SSD_HEADS // 2
SSD_HIST = 8


def _sigmoid(x):
    return 1.0 / (1.0 + jnp.exp(-x))


def _ssd_kernel(pa_ref, dt_ref, cw_ref, cb_ref, dtb_ref, alog_ref, dsk_ref, g_ref, sel_ref,
                y_ref, hlast_ref, ext_ref, state_ref, ybuf_ref, *, n_chunks):
    c = pl.program_id(1)
    ln = SSD_CHUNK
    hi = lax.Precision.HIGHEST

    @pl.when(c == 0)
    def _():
        ext_ref[0:SSD_HIST, :] = jnp.zeros((SSD_HIST, SSD_CONV_DIM), F32)
        state_ref[...] = jnp.zeros_like(state_ref)

    ext_ref[SSD_HIST:SSD_HIST + ln, :] = pa_ref[0, :, OFF_XBC:OFF_DT]
    acc = cb_ref[...]
    for k in range(SSD_CONV):
        off = SSD_HIST - (SSD_CONV - 1) + k
        acc = acc + ext_ref[off:off + ln, :] * cw_ref[k:k + 1, :]
    ext_ref[0:SSD_HIST, :] = ext_ref[ln:ln + SSD_HIST, :]
    xc = acc * _sigmoid(acc)
    gn = SSD_GROUPS * SSD_STATE
    bm = xc[:, D_GROUP:D_GROUP + gn].astype(BF16)
    cm = xc[:, D_GROUP + gn:].astype(BF16)

    dtr = dt_ref[0] + dtb_ref[...]
    dt = jnp.maximum(dtr, 0.0) + jnp.log1p(jnp.exp(-jnp.abs(dtr)))
    da = dt * (-jnp.exp(alog_ref[...]))
    r_id = lax.broadcasted_iota(jnp.int32, (ln, ln), 0)
    c_id = lax.broadcasted_iota(jnp.int32, (ln, ln), 1)
    causal = c_id <= r_id
    acum = jnp.dot(causal.astype(F32), da, preferred_element_type=F32, precision=hi)
    last = acum[ln - 1:ln, :]
    acum_t = acum.T
    dt_t = dt.T
    sel = sel_ref[...]
    w_x = jnp.dot(jnp.exp(last - acum) * dt, sel, preferred_element_type=F32, precision=hi)
    ea_x = jnp.dot(jnp.exp(acum), sel, preferred_element_type=F32, precision=hi)
    lane = lax.broadcasted_iota(jnp.int32, (ln, LANES), 1)
    first_head = lane < SSD_HEADDIM

    scores = []
    for g in range(SSD_GROUPS):
        sl = slice(g * SSD_STATE, (g + 1) * SSD_STATE)
        scores.append(lax.dot_general(cm[:, sl], bm[:, sl], (((1,), (1,)), ((), ())),
                                      preferred_element_type=F32))

    for pr in range(SSD_PAIRS):
        g = (2 * pr) // (SSD_HEADS // SSD_GROUPS)
        gsl = slice(g * SSD_STATE, (g + 1) * SSD_STATE)
        col = slice(pr * LANES, (pr + 1) * LANES)
        x_pair = xc[:, col]
        x_pair_b = x_pair.astype(BF16)
        yd = []
        for hd in (2 * pr, 2 * pr + 1):
            diff = jnp.broadcast_to(acum[:, hd:hd + 1], (ln, ln)) - acum_t[hd:hd + 1, :]
            decay = jnp.exp(jnp.where(causal, diff, -jnp.inf))
            m = (scores[g] * decay) * dt_t[hd:hd + 1, :]
            yd.append(jnp.dot(m.astype(BF16), x_pair_b, preferred_element_type=F32))
        y_diag = jnp.where(first_head, yd[0], yd[1])
        st = state_ref[pr]
        y_off = jnp.dot(cm[:, gsl], st.astype(BF16), preferred_element_type=F32) * ea_x[:, col]
        xw = (x_pair * w_x[:, col]).astype(BF16)
        st_new = lax.dot_general(bm[:, gsl], xw, (((0,), (0,)), ((), ())), preferred_element_type=F32)
        state_ref[pr] = st * ea_x[ln - 1:ln, col] + st_new
        y = (y_diag + y_off) + x_pair * dsk_ref[:, col]
        z = pa_ref[0, :, pr * LANES:(pr + 1) * LANES]
        ybuf_ref[:, col] = y * (z * _sigmoid(z))

    yv = ybuf_ref[...]
    yv = yv * lax.rsqrt(jnp.mean(yv * yv, axis=-1, keepdims=True) + EPS) * g_ref[...]
    y_ref[0] = yv.astype(y_ref.dtype)

    @pl.when(c == n_chunks - 1)
    def _():
        for pr in range(SSD_PAIRS):
            hlast_ref[0, pr] = state_ref[pr].T


def _ssd_prompt(proj_a, proj_dt, p, l):
    b, t, _ = proj_a.shape
    n_chunks = t // SSD_CHUNK
    assert n_chunks * SSD_CHUNK == t
    pad_h = lambda v: jnp.pad(v.astype(F32), (0, LANES - SSD_HEADS)).reshape(1, LANES)
    sel = (jnp.arange(LANES, dtype=jnp.int32)[:, None] ==
           (jnp.arange(D_GROUP, dtype=jnp.int32) // SSD_HEADDIM)[None, :]).astype(F32)
    full = lambda shape: pl.BlockSpec(shape, lambda bi, ci: (0,) * len(shape))
    y, hlast = pl.pallas_call(
        functools.partial(_ssd_kernel, n_chunks=n_chunks),
        name="ssd_mixer",
        grid=(b, n_chunks),
        in_specs=[
            pl.BlockSpec((1, SSD_CHUNK, OFF_DT), lambda bi, ci: (bi, ci, 0)),
            pl.BlockSpec((1, SSD_CHUNK, LANES), lambda bi, ci: (bi, ci, 0)),
            full((SSD_CONV, SSD_CONV_DIM)), full((1, SSD_CONV_DIM)),
            full((1, LANES)), full((1, LANES)), full((1, D_GROUP)), full((1, D_GROUP)),
            full((LANES, D_GROUP)),
        ],
        out_specs=[
            pl.BlockSpec((1, SSD_CHUNK, D_GROUP), lambda bi, ci: (bi, ci, 0)),
            pl.BlockSpec((1, SSD_PAIRS, LANES, SSD_STATE), lambda bi, ci: (bi, 0, 0, 0)),
        ],
        out_shape=[jax.ShapeDtypeStruct((b, t, D_GROUP), BF16),
                   jax.ShapeDtypeStruct((b, SSD_PAIRS, LANES, SSD_STATE), F32)],
        scratch_shapes=[pltpu.VMEM((SSD_HIST + SSD_CHUNK, SSD_CONV_DIM), F32),
                        pltpu.VMEM((SSD_PAIRS, SSD_STATE, LANES), F32),
                        pltpu.VMEM((SSD_CHUNK, D_GROUP), F32)],
        compiler_params=pltpu.CompilerParams(
            dimension_semantics=("arbitrary", "arbitrary"), vmem_limit_bytes=VMEM_LIMIT_BYTES),
    )(proj_a, proj_dt, p['ssd_conv_w'][l], p['ssd_conv_b'][l].reshape(1, SSD_CONV_DIM),
      pad_h(p['ssd_dt_bias'][l]), pad_h(p['ssd_a_log'][l]),
      jnp.repeat(p['ssd_d'][l].astype(F32), SSD_HEADDIM).reshape(1, D_GROUP),
      p['ssd_norm_g'][l].reshape(1, D_GROUP), sel)
    return y.reshape(b * t, D_GROUP), hlast.reshape(b, SSD_HEADS, SSD_HEADDIM, SSD_STATE)


POOL_CHUNK = 256
POOL_HIST = 16


def _pool_kernel(u_ref, w_ref, scale_ref, g_ref, o_ref, ext_ref, ybuf_ref):
    c = pl.program_id(1)
    ln = POOL_CHUNK

    @pl.when(c == 0)
    def _():
        ext_ref[0:POOL_HIST, :] = jnp.zeros((POOL_HIST, D_GROUP), F32)

    ext_ref[POOL_HIST:POOL_HIST + ln, :] = u_ref[0]
    pos = c * ln + lax.broadcasted_iota(jnp.int32, (ln, 1), 0)
    for gi, win in enumerate(POOL_WINDOWS):
        col = slice(gi * POOL_GROUP, (gi + 1) * POOL_GROUP)
        u = ext_ref[POOL_HIST:POOL_HIST + ln, col]
        tot = u
        for k in range(1, win):
            tot = tot + ext_ref[POOL_HIST - k:POOL_HIST - k + ln, col]
        cnt = jnp.minimum(win, pos + 1).astype(F32)
        d = tot / cnt - u
        y = jnp.dot(d.astype(BF16), w_ref[gi].astype(BF16), preferred_element_type=F32)
        ybuf_ref[:, col] = y * scale_ref[:, col]
    ext_ref[0:POOL_HIST, :] = ext_ref[ln:ln + POOL_HIST, :]
    yv = ybuf_ref[...]
    yv = yv * lax.rsqrt(jnp.mean(yv * yv, axis=-1, keepdims=True) + EPS) * g_ref[...]
    o_ref[0] = yv.astype(o_ref.dtype)


def _pool_prompt(proj, p, l, *, col0):
    b, t, _ = proj.shape
    full = lambda shape: pl.BlockSpec(shape, lambda bi, ci: (0,) * len(shape))
    y = pl.pallas_call(
        _pool_kernel,
        name="pool_mixer",
        grid=(b, t // POOL_CHUNK),
        in_specs=[pl.BlockSpec((1, POOL_CHUNK, D_GROUP), lambda bi, ci: (bi, ci, col0)),
                  full((len(POOL_WINDOWS), POOL_GROUP, POOL_GROUP)), full((1, D_GROUP)), full((1, D_GROUP))],
        out_specs=pl.BlockSpec((1, POOL_CHUNK, D_GROUP), lambda bi, ci: (bi, ci, 0)),
        out_shape=jax.ShapeDtypeStruct((b, t, D_GROUP), BF16),
        scratch_shapes=[pltpu.VMEM((POOL_HIST + POOL_CHUNK, D_GROUP), F32),
                        pltpu.VMEM((POOL_CHUNK, D_GROUP), F32)],
        compiler_params=pltpu.CompilerParams(
            dimension_semantics=("arbitrary", "arbitrary"), vmem_limit_bytes=VMEM_LIMIT_BYTES),
    )(proj, p['pool_w'][l], p['pool_scale'][l].reshape(1, D_GROUP), p['pool_norm_g'][l].reshape(1, D_GROUP))
    return y.reshape(b * t, D_GROUP)


def _rmsnorm(x, g):
    xf = x.astype(F32)
    xf = xf * lax.rsqrt(jnp.mean(xf * xf, axis=-1, keepdims=True) + EPS)
    return (xf * g.astype(F32)).astype(x.dtype)


def _alibi_slopes(n):
    return 2.0 ** (-8.0 * jnp.arange(1, n + 1, dtype=F32) / n)


def _causal_depthwise_conv(ext, w, bias, t):
    out = bias
    for k in range(w.shape[0]):
        out = out + ext[:, k:k + t] * w[k]
    return out


def _ssd_chunked_scan(x, dt, a, bm, cm, h0, chunk):
    b, t, h, p = x.shape
    nc = t // chunk
    rep = h // SSD_GROUPS
    bh = jnp.repeat(bm, rep, axis=2).astype(F32)
    ch = jnp.repeat(cm, rep, axis=2).astype(F32)
    da = dt * a
    resh = lambda v: v.reshape((b, nc, chunk) + v.shape[2:])
    xc, dtc, ac, bc, cc = resh(x.astype(F32)), resh(dt), resh(da), resh(bh), resh(ch)
    acum = jnp.cumsum(ac, axis=2)
    diff = acum[:, :, :, None, :] - acum[:, :, None, :, :]
    causal = jnp.tril(jnp.ones((chunk, chunk), dtype=bool))
    decay = jnp.exp(jnp.where(causal[None, None, :, :, None], diff, -jnp.inf))
    scores = jnp.einsum('bcihn,bcjhn->bcijh', cc, bc)
    y_diag = jnp.einsum('bcijh,bcjhp->bcihp', scores * decay * dtc[:, :, None], xc)
    decay_end = jnp.exp(acum[:, :, -1:, :] - acum)
    states = jnp.einsum('bcjhn,bcjh,bcjhp->bchpn', bc, decay_end * dtc, xc)
    chunk_decay = jnp.exp(acum[:, :, -1, :])

    def step(hh, inp):
        dec, st = inp
        return hh * dec[:, :, None, None] + st, hh

    h_last, h_start = lax.scan(step, h0.astype(F32),
                               (jnp.moveaxis(chunk_decay, 1, 0), jnp.moveaxis(states, 1, 0)))
    h_start = jnp.moveaxis(h_start, 0, 1)
    y_off = jnp.einsum('bcihn,bchpn->bcihp', cc, h_start) * jnp.exp(acum)[..., None]
    y = (y_diag + y_off).reshape(b, t, h, p)
    return y.astype(x.dtype), h_last.astype(h0.dtype)


def _ssd_mixer(z, xbc, dt_raw, conv_buf, h0, conv_w, conv_b, dt_bias, a_log, d_skip, norm_g):
    b, t, _ = xbc.shape
    ext = jnp.concatenate([conv_buf.astype(xbc.dtype), xbc], axis=1)
    new_buf = ext[:, t:]
    xbc_c = jax.nn.silu(_causal_depthwise_conv(ext, conv_w, conv_b, t))
    gn = SSD_GROUPS * SSD_STATE
    xs = xbc_c[..., :D_GROUP].reshape(b, t, SSD_HEADS, SSD_HEADDIM)
    bm = xbc_c[..., D_GROUP:D_GROUP + gn].reshape(b, t, SSD_GROUPS, SSD_STATE)
    cm = xbc_c[..., D_GROUP + gn:].reshape(b, t, SSD_GROUPS, SSD_STATE)
    dt = jax.nn.softplus((dt_raw + dt_bias).astype(F32))
    a = -jnp.exp(a_log.astype(F32))
    chunk = SSD_CHUNK if t % SSD_CHUNK == 0 else t
    y, h_last = _ssd_chunked_scan(xs, dt, a, bm, cm, h0, chunk)
    y = (y + xs * d_skip[:, None]).reshape(b, t, D_GROUP)
    y = _rmsnorm(y * jax.nn.silu(z), norm_g)
    return y, new_buf, h_last


def _pool_mixer(u, buf, pos0, pool_w, pool_scale):
    b, t, c = u.shape
    ext = jnp.concatenate([buf.astype(u.dtype), u], axis=1)
    new_buf = ext[:, t:]
    cs = jnp.cumsum(ext.astype(F32), axis=1)
    cs = jnp.concatenate([jnp.zeros_like(cs[:, :1]), cs], axis=1)
    pos = pos0 + jnp.arange(t, dtype=jnp.int32)
    outs = []
    for g, w in enumerate(POOL_WINDOWS):
        sl = slice(g * POOL_GROUP, (g + 1) * POOL_GROUP)
        end = cs[:, POOL_BUF + 1:POOL_BUF + 1 + t, sl]
        start = cs[:, POOL_BUF + 1 - w:POOL_BUF + 1 - w + t, sl]
        cnt = jnp.minimum(w, pos + 1).astype(F32)[None, :, None]
        d = ((end - start) / cnt - u[..., sl].astype(F32)).astype(u.dtype)
        outs.append(d @ pool_w[g])
    return jnp.concatenate(outs, axis=-1) * pool_scale, new_buf


def _s5_mixer(u, h0_re, h0_im, a_re, a_im, log_dt, b_re, b_im, c_re, c_im, d_skip, glu_w, glu_b):
    b, t, c = u.shape
    ug = u.reshape(b, t, S5_GROUPS, S5_CH).astype(F32)
    dt = jnp.exp(log_dt.astype(F32))[:, None]
    ar, ai = a_re.astype(F32), a_im.astype(F32)
    mag = jnp.exp(dt * ar)
    abar_re, abar_im = mag * jnp.cos(dt * ai), mag * jnp.sin(dt * ai)
    den = ar * ar + ai * ai
    nr, ni = abar_re - 1.0, abar_im
    f_re, f_im = (nr * ar + ni * ai) / den, (ni * ar - nr * ai) / den
    bb_re = f_re[..., None] * b_re - f_im[..., None] * b_im
    bb_im = f_re[..., None] * b_im + f_im[..., None] * b_re
    bu_re = jnp.einsum('gnc,btgc->btgn', bb_re, ug)
    bu_im = jnp.einsum('gnc,btgc->btgn', bb_im, ug)
    a_full_re = jnp.broadcast_to(abar_re, bu_re.shape)
    a_full_im = jnp.broadcast_to(abar_im, bu_im.shape)

    def combine(e1, e2):
        a1r, a1i, b1r, b1i = e1
        a2r, a2i, b2r, b2i = e2
        return (a2r * a1r - a2i * a1i, a2r * a1i + a2i * a1r,
                a2r * b1r - a2i * b1i + b2r, a2r * b1i + a2i * b1r + b2i)

    _, _, hr, hi = lax.associative_scan(combine, (a_full_re, a_full_im, bu_re, bu_im), axis=1)
    k = jnp.arange(1, t + 1, dtype=F32)[:, None, None]
    pm = jnp.exp(k * dt * ar)
    pw_re, pw_im = pm * jnp.cos(k * dt * ai), pm * jnp.sin(k * dt * ai)
    h0r, h0i = h0_re.astype(F32)[:, None], h0_im.astype(F32)[:, None]
    hr = hr + pw_re * h0r - pw_im * h0i
    hi = hi + pw_re * h0i + pw_im * h0r
    y = jnp.einsum('gcn,btgn->btgc', c_re, hr) - jnp.einsum('gcn,btgn->btgc', c_im, hi)
    y = y.reshape(b, t, c) + d_skip * u.astype(F32)
    y = jax.nn.gelu(y)
    y = y * jax.nn.sigmoid(y @ glu_w.astype(F32) + glu_b)
    return y.astype(u.dtype), hr[:, -1].astype(h0_re.dtype), hi[:, -1].astype(h0_im.dtype)


def _moba_attend(q, k_all, v_all, q_pos):
    b, tq, h, dh = q.shape
    l = k_all.shape[1]
    nb = -(-l // MOBA_BLOCK)
    pad = nb * MOBA_BLOCK - l
    padw = ((0, 0), (0, pad), (0, 0), (0, 0))
    kb = jnp.pad(k_all, padw).reshape(b, nb, MOBA_BLOCK, h, dh).transpose(0, 3, 1, 2, 4)
    vb = jnp.pad(v_all, padw).reshape(b, nb, MOBA_BLOCK, h, dh).transpose(0, 3, 1, 2, 4)
    kmean = jnp.mean(kb.astype(F32), axis=3)
    slopes = _alibi_slopes(h)
    scale = 1.0 / math.sqrt(dh)
    bi = jnp.arange(b)[:, None, None, None]
    hi = jnp.arange(h)[None, :, None, None]
    blk_ids = jnp.arange(nb, dtype=jnp.int32)
    offs = jnp.arange(MOBA_BLOCK, dtype=jnp.int32)

    def attend_chunk(args):
        qc, pc = args
        n_q = pc.shape[0]
        own = pc // MOBA_BLOCK
        gate = jnp.einsum('bhqd,bhnd->bhqn', qc.astype(F32), kmean)
        gate = jnp.where(blk_ids[None, :] < own[:, None], gate, -jnp.inf)
        if nb < MOBA_TOPK:
            gate = jnp.pad(gate, ((0, 0), (0, 0), (0, 0), (0, MOBA_TOPK - nb)), constant_values=-jnp.inf)
        g_val, g_idx = lax.top_k(gate, MOBA_TOPK)
        sel = jnp.concatenate([jnp.minimum(g_idx, nb - 1),
                               jnp.broadcast_to(own[None, None, :, None], (b, h, n_q, 1))], axis=-1)
        sel_ok = jnp.concatenate([jnp.isfinite(g_val), jnp.ones((b, h, n_q, 1), dtype=bool)], axis=-1)
        kg = kb[bi, hi, sel]
        vg = vb[bi, hi, sel]
        s = jnp.einsum('bhqd,bhqsnd->bhqsn', qc, kg).astype(F32) * scale
        kpos = sel[..., None] * MOBA_BLOCK + offs
        qpos = pc[None, None, :, None, None]
        s = s - slopes[None, :, None, None, None] * (qpos - kpos).astype(F32)
        s = jnp.where(sel_ok[..., None] & (kpos <= qpos), s, -jnp.inf)
        p = jax.nn.softmax(s.reshape(b, h, n_q, -1), axis=-1).reshape(s.shape)
        return jnp.einsum('bhqsn,bhqsnd->bhqd', p.astype(vg.dtype), vg)

    qlen = MOBA_QCHUNK if tq % MOBA_QCHUNK == 0 else tq
    nq = tq // qlen
    qs = jnp.moveaxis(q.transpose(0, 2, 1, 3).reshape(b, h, nq, qlen, dh), 2, 0)
    ps = q_pos.reshape(nq, qlen)
    o = lax.map(attend_chunk, (qs, ps))
    return jnp.moveaxis(o, 0, 2).reshape(b, h, tq, dh).transpose(0, 2, 1, 3)


def _moba_mixer(q, k, v, k_past, v_past, pos0, q_g, k_g):
    b, t, _ = q.shape
    shp = (b, t, ATTN_HEADS, ATTN_HEADDIM)
    q = _rmsnorm(q.reshape(shp), q_g)
    k = _rmsnorm(k.reshape(shp), k_g)
    v = v.reshape(shp)
    if k_past is None:
        k_all, v_all = k, v
    else:
        k_all = jnp.concatenate([k_past.astype(k.dtype), k], axis=1)
        v_all = jnp.concatenate([v_past.astype(v.dtype), v], axis=1)
    o = _moba_attend(q, k_all, v_all, pos0 + jnp.arange(t, dtype=jnp.int32))
    return o.reshape(b, t, D_GROUP), k, v


GATHER_TILE = 256


def _gather_rows_kernel(idx_ref, src_ref, o_ref, sem):
    base = pl.program_id(0) * GATHER_TILE

    def row_copy(r):
        return pltpu.make_async_copy(src_ref.at[pl.ds(idx_ref[base + r], 1), :], o_ref.at[pl.ds(r, 1), :], sem)

    def start(r, carry):
        row_copy(r).start()
        return carry

    def wait(r, carry):
        row_copy(r).wait()
        return carry

    lax.fori_loop(0, GATHER_TILE, start, 0)
    lax.fori_loop(0, GATHER_TILE, wait, 0)


def _gather_rows(src, idx):
    n_out = idx.shape[0]
    assert n_out % GATHER_TILE == 0
    cols = src.shape[1]
    return pl.pallas_call(
        _gather_rows_kernel,
        name="gather_rows",
        grid_spec=pltpu.PrefetchScalarGridSpec(
            num_scalar_prefetch=1,
            grid=(n_out // GATHER_TILE,),
            in_specs=[pl.BlockSpec(memory_space=pl.ANY)],
            out_specs=pl.BlockSpec((GATHER_TILE, cols), lambda i, idx_ref: (i, 0)),
            scratch_shapes=[pltpu.SemaphoreType.DMA(())],
        ),
        out_shape=jax.ShapeDtypeStruct((n_out, cols), src.dtype),
        compiler_params=pltpu.CompilerParams(
            dimension_semantics=("arbitrary",), vmem_limit_bytes=VMEM_LIMIT_BYTES),
    )(idx, src)


def _dense_ffn(h2, x_res, g2, rows_per_gate, w1, w3, w2, l2, *, tm):
    rows = h2.shape[0]
    n_tiles = rows // tm
    grp = jnp.full((n_tiles,), l2, jnp.int32)
    act = _gmm(h2, [w1, w3], grp, tm=tm, tn=512, out_dtype=BF16, name="ffn_up")
    w2h = w2.reshape(w2.shape[0] * 2, D_FF_EXPERT, D_MODEL)
    part = _gmm(act, [w2h], grp * 2, tm=tm, tn=512, out_dtype=F32, x_col_block=0, k_dim=D_FF_EXPERT,
                name="ffn_down_lo")
    return _gmm(act, [w2h], grp * 2 + 1, tm=tm, tn=512, out_dtype=F32, x_col_block=1, k_dim=D_FF_EXPERT,
                name="ffn_down_hi", partial=part, gate=g2, rows_per_gate=rows_per_gate, res=x_res)


def _moe_ffn(h2, logits, router_b, w1, w3, w2, l2, *, tm):
    n_tok = h2.shape[0]
    lg = logits[:, :N_EXPERTS] + router_b.astype(F32)
    top_v, top_i = lax.top_k(lg, TOP_K)
    gates = jax.nn.softmax(top_v, axis=-1)

    n_pairs = n_tok * TOP_K
    n_tiles = -(-n_pairs // tm) + N_EXPERTS
    p_rows = n_tiles * tm
    e_flat = top_i.reshape(-1).astype(jnp.int32)
    onehot = (e_flat[:, None] == jnp.arange(N_EXPERTS, dtype=jnp.int32)[None, :]).astype(jnp.int32)
    rank = jnp.sum((jnp.cumsum(onehot, axis=0) - onehot) * onehot, axis=1)
    counts = jnp.sum(onehot, axis=0)
    tiles_e = (counts + tm - 1) // tm
    tile_end = jnp.cumsum(tiles_e)
    starts = (tile_end - tiles_e) * tm
    pos = starts[e_flat] + rank
    tok = jnp.arange(n_pairs, dtype=jnp.int32) // TOP_K
    row_token = jnp.zeros((p_rows,), jnp.int32).at[pos].set(tok)
    row_gate = jnp.zeros((p_rows,), F32).at[pos].set(gates.reshape(-1))
    tile_ids = jnp.arange(n_tiles, dtype=jnp.int32)
    total_tiles = tile_end[-1]
    tile_valid = (tile_ids < total_tiles).astype(jnp.int32)
    clipped = jnp.minimum(tile_ids, total_tiles - 1)
    tile_expert = jnp.sum((clipped[:, None] >= tile_end[None, :]).astype(jnp.int32), axis=1)
    tile_group = (l2 * N_EXPERTS + tile_expert).astype(jnp.int32)

    xs = _gather_rows(h2, row_token).astype(BF16)
    n_moe = w1.shape[0]
    w1r = w1.reshape(n_moe * N_EXPERTS, D_MODEL, D_FF_EXPERT)
    w3r = w3.reshape(n_moe * N_EXPERTS, D_MODEL, D_FF_EXPERT)
    w2r = w2.reshape(n_moe * N_EXPERTS, D_FF_EXPERT, D_MODEL)
    act = _gmm(xs, [w1r, w3r], tile_group, tm=tm, tn=512, out_dtype=BF16, name="moe_up", tile_valid=tile_valid)
    ys = _gmm(act, [w2r], tile_group, tm=tm, tn=512, out_dtype=F32, name="moe_down",
              rowscale=row_gate.reshape(p_rows, 1), tile_valid=tile_valid)
    pos2 = pos.reshape(n_tok, TOP_K)
    n_pad = -(-n_tok // GATHER_TILE) * GATHER_TILE
    pos2 = jnp.pad(pos2, ((0, n_pad - n_tok), (0, 0)))
    return _gather_rows(ys, pos2[:, 0])[:n_tok] + _gather_rows(ys, pos2[:, 1])[:n_tok]


def _layer(l, xr, b, t, mod, pos0, p, cache, page_table):
    rows = b * t
    tm = 512 if rows % 512 == 0 else rows
    tr = 256 if rows % 256 == 0 else rows
    n_tiles = rows // tm
    if True:
        if cache is None:
            k_past = v_past = None
        else:
            cache_k, cache_v, st_ssd, st_conv, st_pool, st_re, st_im = cache
            n_pool = cache_k.shape[1]
            pages = page_table + l * n_pool
            k_past = cache_k.reshape((DEPTH * n_pool,) + cache_k.shape[2:])[pages]
            v_past = cache_v.reshape((DEPTH * n_pool,) + cache_v.shape[2:])[pages]
            k_past = k_past.reshape(b, -1, ATTN_HEADS, ATTN_HEADDIM)
            v_past = v_past.reshape(b, -1, ATTN_HEADS, ATTN_HEADDIM)
            ssd_h0, conv_buf, pool_buf = st_ssd[l], st_conv[l], st_pool[l]
            s5_h0_re, s5_h0_im = st_re[l], st_im[l]
        sh1, sc1, g1, sh2, sc2, g2 = [m.reshape(b, 1, D_MODEL) for m in jnp.split(mod, 6, axis=-1)]
        if t >= tm:
            gate_rows = t
            as_gate = lambda m: m
        else:
            gate_rows = rows
            as_gate = lambda m: jnp.broadcast_to(m, (b, t, D_MODEL)).reshape(1, rows, D_MODEL)
        grp = jnp.full((n_tiles,), l, jnp.int32)

        h = _norm_mod(xr, p['norm1_g'][l], as_gate(sc1), as_gate(sh1), tr=tr, rows_per_gate=gate_rows)
        proj_a = _gmm(h, [p['w_in']], grp, tm=tm, tn=512, out_dtype=F32, n_cols=OFF_DT, name="proj_a")
        proj_b = _gmm(h, [p['w_in_b']], grp, tm=tm, tn=512, out_dtype=F32, name="proj_b")
        proj_dt = _gmm(h, [p['w_in_dt']], grp, tm=tm, tn=LANES, out_dtype=F32, name="proj_dt")
        proj_a = proj_a.reshape(b, t, OFF_DT)
        proj_b = proj_b.reshape(b, t, D_IN - OFF_POOL)
        proj_dt = proj_dt.reshape(b, t, LANES)
        seg = lambda i: proj_b[..., i * D_GROUP:(i + 1) * D_GROUP]
        to_rows = lambda v: v.reshape(rows, D_GROUP).astype(BF16)

        if cache is None:
            y_ssdn, ssd_new = _ssd_prompt(proj_a, proj_dt, p, l)
            conv_new = proj_a[:, t - (SSD_CONV - 1):, OFF_XBC:OFF_DT]
            y_pooln = _pool_prompt(proj_b, p, l, col0=0)
            pool_new = proj_b[:, t - POOL_BUF:, :D_GROUP]
            y_s5n, re_new, im_new = _s5_prompt(proj_b, p, l, col0=D_GROUP // S5_CH_BLOCK)
            y_att, k_new = _moba_prompt(proj_b, p['attn_q_g'][l], p['attn_k_g'][l], col0=2)
            k_new = k_new.reshape(b, t, ATTN_HEADS, ATTN_HEADDIM)
            v_new = seg(4).reshape(b, t, ATTN_HEADS, ATTN_HEADDIM)
        else:
            y_ssd, conv_new, ssd_new = _ssd_mixer(
                proj_a[..., OFF_Z:OFF_XBC], proj_a[..., OFF_XBC:OFF_DT], proj_dt[..., :SSD_HEADS],
                conv_buf, ssd_h0, p['ssd_conv_w'][l], p['ssd_conv_b'][l], p['ssd_dt_bias'][l],
                p['ssd_a_log'][l], p['ssd_d'][l], p['ssd_norm_g'][l])
            y_ssdn = to_rows(y_ssd)
            y_pool, pool_new = _pool_mixer(seg(0), pool_buf, pos0, p['pool_w'][l], p['pool_scale'][l])
            y_pooln = to_rows(_rmsnorm(y_pool, p['pool_norm_g'][l]))
            y_s5, re_new, im_new = _s5_mixer(
                seg(1), s5_h0_re, s5_h0_im, p['s5_a_re'][l], p['s5_a_im'][l],
                p['s5_log_dt'][l], p['s5_b_re'][l], p['s5_b_im'][l], p['s5_c_re'][l], p['s5_c_im'][l],
                p['s5_d'][l], p['s5_glu_w'][l], p['s5_glu_b'][l])
            y_s5n = _rmsnorm(y_s5, p['s5_norm_g'][l]).reshape(rows, D_GROUP).astype(BF16)
            y_att, k_new, v_new = _moba_mixer(seg(2), seg(3), seg(4), k_past, v_past, pos0,
                                              p['attn_q_g'][l], p['attn_k_g'][l])
        mixed_in = jnp.concatenate([y_ssdn,
                                    y_pooln,
                                    y_s5n,
                                    to_rows(_rmsnorm(y_att, p['attn_norm_g'][l]))], axis=-1)
        xr = _gmm(mixed_in, [p['w_out']], grp, tm=tm, tn=512, out_dtype=F32, name="w_out",
                  gate=as_gate(g1), rows_per_gate=gate_rows, res=xr)

        states = (k_new, v_new, ssd_new, conv_new, pool_new, re_new, im_new)
        if l % 2 == 0:
            h2 = _norm_mod(xr, p['norm2_g'][l], as_gate(sc2), as_gate(sh2), tr=tr, rows_per_gate=gate_rows)
            xr = _dense_ffn(h2, xr, as_gate(g2), gate_rows, p['ffn_w1'], p['ffn_w3'], p['ffn_w2'], l // 2, tm=tm)
            return xr, states, None
        h2, logits = _norm_mod(xr, p['norm2_g'][l], as_gate(sc2), as_gate(sh2), tr=tr,
                               rows_per_gate=gate_rows, router_w=p['moe_router_w'][l // 2])
        g2_rows = jnp.broadcast_to(g2, (b, t, D_MODEL)).reshape(rows, D_MODEL)
        return xr, states, (h2, logits, g2_rows)


def kernel(x_prompt, x_sample, cache_k, cache_v, state_ssd, state_ssd_conv, state_pool,
           state_s5_re, state_s5_im, page_table, c_prompt, c_sample,
           ada_w, ada_b, norm1_g, norm2_g, w_in, w_out,
           ssd_conv_w, ssd_conv_b, ssd_dt_bias, ssd_a_log, ssd_d, ssd_norm_g,
           pool_w, pool_scale, pool_norm_g,
           s5_a_re, s5_a_im, s5_log_dt, s5_b_re, s5_b_im, s5_c_re, s5_c_im,
           s5_d, s5_glu_w, s5_glu_b, s5_norm_g,
           attn_q_g, attn_k_g, attn_norm_g,
           ffn_w1, ffn_w3, ffn_w2,
           moe_router_w, moe_router_b, moe_w1, moe_w3, moe_w2):
    w_in_b = w_in[:, :, OFF_POOL:]
    w_in_dt = jnp.pad(w_in[:, :, OFF_DT:OFF_POOL], ((0, 0), (0, 0), (0, LANES - SSD_HEADS)))
    p = dict(norm1_g=norm1_g, norm2_g=norm2_g, w_in=w_in, w_in_b=w_in_b, w_in_dt=w_in_dt, w_out=w_out,
             ssd_conv_w=ssd_conv_w, ssd_conv_b=ssd_conv_b, ssd_dt_bias=ssd_dt_bias,
             ssd_a_log=ssd_a_log, ssd_d=ssd_d, ssd_norm_g=ssd_norm_g,
             pool_w=pool_w, pool_scale=pool_scale, pool_norm_g=pool_norm_g,
             s5_a_re=s5_a_re, s5_a_im=s5_a_im, s5_log_dt=s5_log_dt, s5_b_re=s5_b_re, s5_b_im=s5_b_im,
             s5_c_re=s5_c_re, s5_c_im=s5_c_im, s5_d=s5_d, s5_glu_w=s5_glu_w, s5_glu_b=s5_glu_b,
             s5_norm_g=s5_norm_g, attn_q_g=attn_q_g, attn_k_g=attn_k_g, attn_norm_g=attn_norm_g,
             ffn_w1=ffn_w1, ffn_w3=ffn_w3, ffn_w2=ffn_w2,
             moe_router_w=moe_router_w, moe_router_b=moe_router_b,
             moe_w1=moe_w1, moe_w3=moe_w3, moe_w2=moe_w2)

    c_all = jnp.concatenate([c_prompt, c_sample], axis=0)
    n_c = c_all.shape[0]
    c_pad = jnp.pad(jax.nn.silu(c_all), ((0, BF16_SUBLANES - n_c), (0, 0))).astype(BF16)
    mods_p, mods_s = [], []
    for l in range(DEPTH):
        m = _gmm(c_pad, [ada_w], jnp.full((1,), l, jnp.int32), tm=BF16_SUBLANES, tn=512, out_dtype=F32,
                 name="ada_mod")
        m = m[:n_c] + ada_b[l]
        mods_p.append(m[:BATCH])
        mods_s.append(m[BATCH:])

    cache = (cache_k, cache_v, state_ssd, state_ssd_conv, state_pool, state_s5_re, state_s5_im)
    n_p, n_s = BATCH * SEQ, DEC_BATCH * DEC_SEQ
    xp = x_prompt.reshape(n_p, D_MODEL)
    xs = x_sample.reshape(n_s, D_MODEL)
    new_p, new_s = [[] for _ in range(7)], [[] for _ in range(7)]
    for l in range(DEPTH):
        xp, st_p, moe_p = _layer(l, xp, BATCH, SEQ, mods_p[l], 0, p, None, None)
        xs, st_s, moe_s = _layer(l, xs, DEC_BATCH, DEC_SEQ, mods_s[l], PAST_LEN, p, cache, page_table)
        if moe_p is not None:
            h2 = jnp.concatenate([moe_p[0], moe_s[0]], axis=0)
            logits = jnp.concatenate([moe_p[1], moe_s[1]], axis=0)
            f = _moe_ffn(h2, logits, moe_router_b[l // 2], moe_w1, moe_w3, moe_w2, l // 2, tm=512)
            xp = xp + moe_p[2] * f[:n_p]
            xs = xs + moe_s[2] * f[n_p:]
        for lst, val in zip(new_p, st_p):
            lst.append(val)
        for lst, val in zip(new_s, st_s):
            lst.append(val)
    y_prompt = xp.reshape(BATCH, SEQ, D_MODEL)
    y_sample = xs.reshape(DEC_BATCH, DEC_SEQ, D_MODEL)
    k_p, v_p, ssd_p, conv_p, pool_p, re_p, im_p = [jnp.stack(v, axis=0) for v in new_p]
    k_s, v_s, ssd_s, conv_s, pool_s, re_s, im_s = [jnp.stack(v, axis=0) for v in new_s]
    return (y_prompt, y_sample, k_p, v_p, ssd_p, conv_p, pool_p, re_p, im_p,
            k_s, v_s, ssd_s, conv_s, pool_s, re_s, im_s)
```
